```python
import math
import jax, jax.numpy as jnp
from jax import lax
import numpy as np

D_MODEL = 2048
BATCH = 2
SEQ = 4096
DEPTH = 4
DEC_BATCH = 1
DEC_SEQ = 8192
PAST_LEN = 128

N_EVEN = (DEPTH + 1) // 2
N_ODD = DEPTH // 2
D_FF = 5632
NORM_EPS = 1e-6
HYENA_WIDTH = D_MODEL // 2
LRU_WIDTH = D_MODEL // 2
IN_EVEN = 3 * HYENA_WIDTH + 2 * LRU_WIDTH
HYENA_SHORT = 3
HYENA_EMB = 33
HYENA_BANDS = (HYENA_EMB - 1) // 2
HYENA_FILTER_ORDER = 64
HYENA_INNER = 2
HYENA_TARGET = 1e-2
HYENA_FAST = 0.3
HYENA_SLOW = 1.5
LRU_BLOCKS = 8
LRU_BLOCK = LRU_WIDTH // LRU_BLOCKS
LRU_CONV = 4
LRU_C = 8.0
ATT_HEADS = 16
ATT_HEAD_DIM = D_MODEL // (2 * ATT_HEADS)
ATT_V_DIM = 2 * ATT_HEAD_DIM
Q_BLOCK = 128
ROPE_THETA = 10000.0

kernel_name = "hyena_rglru_diffattn_macaron_encoder"


def _rmsnorm(x, g):
    xf = x.astype(jnp.float32)
    y = xf * lax.rsqrt(jnp.mean(xf * xf, axis=-1, keepdims=True) + NORM_EPS)
    return (y * g.astype(jnp.float32)).astype(x.dtype)


def _swiglu(x, w1, w3, w2):
    return (jax.nn.silu(x @ w1) * (x @ w3)) @ w2


def _depthwise_conv(x, w, b, left):
    K = w.shape[0]
    L = x.shape[1]
    xp = jnp.pad(x, ((0, 0), (left, K - 1 - left), (0, 0)))
    y = b
    for j in range(K):
        y = y + xp[:, j:j + L] * w[j]
    return y


def _hyena_filter(L, w_in, b, w_hid, freq, w_out):
    f32 = jnp.float32
    pos = jnp.arange(L, dtype=f32)[:, None]
    t = pos / max(L - 1, 1)
    w = (2.0 * math.pi / L) * pos
    bands = jnp.linspace(1e-4, HYENA_BANDS - 1, HYENA_BANDS, dtype=f32)[None, :]
    z = jnp.concatenate([t, jnp.cos(bands * w), -jnp.sin(bands * w)], axis=-1)
    fr = freq.astype(f32)
    h = jnp.sin(fr * (z @ w_in.astype(f32) + b[0].astype(f32)))
    for j in range(HYENA_INNER):
        h = jnp.sin(fr * (h @ w_hid[j].astype(f32) + b[j + 1].astype(f32)))
    k = (h @ w_out.astype(f32)).reshape(L, 2, HYENA_WIDTH)
    deltas = jnp.abs(jnp.linspace(math.log(HYENA_TARGET) / HYENA_SLOW,
                                  math.log(HYENA_TARGET) / HYENA_FAST, HYENA_WIDTH, dtype=f32))
    decay = jnp.exp(-t * deltas[None, :])
    k = k * decay[:, None, :]
    return jnp.concatenate([k[:, 0], jnp.zeros((1, HYENA_WIDTH), f32), jnp.flip(k[1:, 1], axis=0)], axis=0)


def _fft_long_conv(u, k_two, bias):
    L = u.shape[1]
    uf = u.astype(jnp.float32)
    U = jnp.fft.rfft(uf, n=2 * L, axis=1)
    Kf = jnp.fft.rfft(k_two, n=2 * L, axis=0)
    y = jnp.fft.irfft(U * Kf[None], n=2 * L, axis=1)[:, :L]
    return (y + uf * bias.astype(jnp.float32)).astype(u.dtype)


def _hyena(p, conv_w, conv_b, f_w_in, f_b, f_w_hid, f_freq, f_w_out, h_bias):
    L = p.shape[1]
    uc = _depthwise_conv(p, conv_w, conv_b, (HYENA_SHORT - 1) // 2)
    x0 = uc[..., :HYENA_WIDTH]
    x1 = uc[..., HYENA_WIDTH:2 * HYENA_WIDTH]
    v = uc[..., 2 * HYENA_WIDTH:]
    k_two = _hyena_filter(L, f_w_in, f_b, f_w_hid, f_freq, f_w_out)
    return x0 * _fft_long_conv(v * x1, k_two, h_bias)


def _lin_combine(e1, e2):
    a1, b1 = e1
    a2, b2 = e2
    return a1 * a2, a2 * b1 + b2


def _rg_lru_direction(xb, wa, ba, wx, bx, lam, reverse):
    B, L, W = xb.shape
    f32 = jnp.float32
    xf = xb.astype(f32)
    xr = xf.reshape(B, L, LRU_BLOCKS, LRU_BLOCK)
    r = jax.nn.sigmoid(jnp.einsum('blnc,ncd->blnd', xr, wa.astype(f32)).reshape(B, L, W) + ba.astype(f32))
    i = jax.nn.sigmoid(jnp.einsum('blnc,ncd->blnd', xr, wx.astype(f32)).reshape(B, L, W) + bx.astype(f32))
    log_a = -LRU_C * r * jax.nn.softplus(-lam.astype(f32))
    a = jnp.exp(log_a)
    b = jnp.sqrt(-jnp.expm1(2.0 * log_a)) * (i * xf)
    _, h = lax.associative_scan(_lin_combine, (a, b), reverse=reverse, axis=1)
    return h


def _rg_lru_block(pg, pl, conv_w, conv_b, wa, ba, wx, bx, lam):
    gate = jax.nn.gelu(pg)
    xb = _depthwise_conv(pl, conv_w, conv_b, LRU_CONV // 2)
    h = (_rg_lru_direction(xb, wa[0], ba[0], wx[0], bx[0], lam[0], False)
         + _rg_lru_direction(xb, wa[1], ba[1], wx[1], bx[1], lam[1], True))
    return h.astype(pg.dtype) * gate


def _rope(L, dim, dtype):
    inv = ROPE_THETA ** (-jnp.arange(0, dim, 2, dtype=jnp.float32) / dim)
    ang = jnp.arange(L, dtype=jnp.float32)[:, None] * inv[None, :]
    ang = jnp.concatenate([ang, ang], axis=-1)
    return jnp.cos(ang).astype(dtype), jnp.sin(ang).astype(dtype)


def _apply_rope(x, cos, sin):
    half = x.shape[-1] // 2
    rot = jnp.concatenate([-x[..., half:], x[..., :half]], axis=-1)
    return x * cos + rot * sin


def _diff_attention(x, w_qkv, lam_vec, subln_g, w_o, lambda_init):
    B, L, _ = x.shape
    qkv = x @ w_qkv
    q = qkv[..., :D_MODEL].reshape(B, L, ATT_HEADS, 2, ATT_HEAD_DIM)
    k = qkv[..., D_MODEL:2 * D_MODEL].reshape(B, L, ATT_HEADS, 2, ATT_HEAD_DIM)
    v = qkv[..., 2 * D_MODEL:].reshape(B, L, ATT_HEADS, ATT_V_DIM)
    cos, sin = _rope(L, ATT_HEAD_DIM, x.dtype)
    cos = cos[None, :, None, None, :]
    sin = sin[None, :, None, None, :]
    q = _apply_rope(q, cos, sin) * (ATT_HEAD_DIM ** -0.5)
    k = _apply_rope(k, cos, sin)
    lv = lam_vec.astype(jnp.float32)
    lam = jnp.exp(jnp.sum(lv[0] * lv[1])) - jnp.exp(jnp.sum(lv[2] * lv[3])) + lambda_init
    nb = L // Q_BLOCK
    qb = jnp.moveaxis(q.reshape(B, nb, Q_BLOCK, ATT_HEADS, 2, ATT_HEAD_DIM), 1, 0)

    def block(qblk):
        s = jnp.einsum('bqhcd,bkhcd->bhcqk', qblk, k).astype(jnp.float32)
        p = jax.nn.softmax(s, axis=-1)
        a = p[:, :, 0] - lam * p[:, :, 1]
        return jnp.einsum('bhqk,bkhe->bqhe', a.astype(v.dtype), v)

    o = jnp.moveaxis(lax.map(block, qb), 0, 1).reshape(B, L, ATT_HEADS, ATT_V_DIM)
    o = _rmsnorm(o, subln_g) * (1.0 - lambda_init)
    return o.reshape(B, L, ATT_HEADS * ATT_V_DIM) @ w_o


def _trunk(x, ffn_norm, ffn_w1, ffn_w3, ffn_w2, mix_norm, final_norm,
           even_w_in, hyena_conv_w, hyena_conv_b, hyena_filt_w_in, hyena_filt_b, hyena_filt_w_hid,
           hyena_filt_freq, hyena_filt_w_out, hyena_bias, lru_conv_w, lru_conv_b, lru_wa, lru_ba,
           lru_wx, lru_bx, lru_lambda, even_w_out, attn_w_qkv, attn_lambda, attn_subln, attn_w_o):
    for i in range(DEPTH):
        x = x + 0.5 * _swiglu(_rmsnorm(x, ffn_norm[i, 0]), ffn_w1[i, 0], ffn_w3[i, 0], ffn_w2[i, 0])
        h = _rmsnorm(x, mix_norm[i])
        j = i // 2
        if i % 2 == 0:
            p = h @ even_w_in[j]
            ph = p[..., :3 * HYENA_WIDTH]
            pg = p[..., 3 * HYENA_WIDTH:3 * HYENA_WIDTH + LRU_WIDTH]
            pl = p[..., 3 * HYENA_WIDTH + LRU_WIDTH:]
            yh = _hyena(ph, hyena_conv_w[j], hyena_conv_b[j], hyena_filt_w_in[j], hyena_filt_b[j],
                        hyena_filt_w_hid[j], hyena_filt_freq[j], hyena_filt_w_out[j], hyena_bias[j])
            yl = _rg_lru_block(pg, pl, lru_conv_w[j], lru_conv_b[j], lru_wa[j], lru_ba[j],
                               lru_wx[j], lru_bx[j], lru_lambda[j])
            x = x + jnp.concatenate([yh, yl], axis=-1) @ even_w_out[j]
        else:
            lambda_init = 0.8 - 0.6 * math.exp(-0.3 * i)
            x = x + _diff_attention(h, attn_w_qkv[j], attn_lambda[j], attn_subln[j], attn_w_o[j], lambda_init)
        x = x + 0.5 * _swiglu(_rmsnorm(x, ffn_norm[i, 1]), ffn_w1[i, 1], ffn_w3[i, 1], ffn_w2[i, 1])
    return _rmsnorm(x, final_norm)


def setup_inputs(seed: int = 0) -> dict:
    key = jax.random.key(seed)
    ks = jax.random.split(key, 32)
    f32 = jnp.float32

    def nrm(k, shape, scale):
        return jax.random.normal(k, shape, f32) * scale

    def gain(k, shape):
        return 1.0 + 0.05 * jax.random.normal(k, shape, f32)

    u = jax.random.uniform(ks[22], (N_EVEN, 2, LRU_WIDTH), f32, 0.9, 0.999)
    a0 = u ** (1.0 / LRU_C)
    lru_lambda = jnp.log(a0) - jnp.log1p(-a0)
    return {
        "x_prompt": nrm(ks[0], (BATCH, SEQ, D_MODEL), 1.0),
        "x_sample": nrm(ks[1], (DEC_BATCH, DEC_SEQ, D_MODEL), 1.0),
        "ffn_norm": gain(ks[2], (DEPTH, 2, D_MODEL)),
        "ffn_w1": nrm(ks[3], (DEPTH, 2, D_MODEL, D_FF), D_MODEL ** -0.5),
        "ffn_w3": nrm(ks[4], (DEPTH, 2, D_MODEL, D_FF), D_MODEL ** -0.5),
        "ffn_w2": nrm(ks[5], (DEPTH, 2, D_FF, D_MODEL), D_FF ** -0.5),
        "mix_norm": gain(ks[6], (DEPTH, D_MODEL)),
        "final_norm": gain(ks[7], (D_MODEL,)),
        "even_w_in": nrm(ks[8], (N_EVEN, D_MODEL, IN_EVEN), D_MODEL ** -0.5),
        "hyena_conv_w": nrm(ks[9], (N_EVEN, HYENA_SHORT, 3 * HYENA_WIDTH), HYENA_SHORT ** -0.5),
        "hyena_conv_b": nrm(ks[10], (N_EVEN, 3 * HYENA_WIDTH), 0.02),
        "hyena_filt_w_in": nrm(ks[11], (N_EVEN, HYENA_EMB, HYENA_FILTER_ORDER), HYENA_EMB ** -0.5),
        "hyena_filt_b": nrm(ks[12], (N_EVEN, HYENA_INNER + 1, HYENA_FILTER_ORDER), 0.1),
        "hyena_filt_w_hid": nrm(ks[13], (N_EVEN, HYENA_INNER, HYENA_FILTER_ORDER, HYENA_FILTER_ORDER), HYENA_FILTER_ORDER ** -0.5),
        "hyena_filt_freq": gain(ks[14], (N_EVEN, HYENA_FILTER_ORDER)),
        "hyena_filt_w_out": nrm(ks[15], (N_EVEN, HYENA_FILTER_ORDER, 2 * HYENA_WIDTH), 0.1 * HYENA_FILTER_ORDER ** -0.5),
        "hyena_bias": nrm(ks[16], (N_EVEN, HYENA_WIDTH), 0.5),
        "lru_conv_w": nrm(ks[17], (N_EVEN, LRU_CONV, LRU_WIDTH), LRU_CONV ** -0.5),
        "lru_conv_b": nrm(ks[18], (N_EVEN, LRU_WIDTH), 0.02),
        "lru_wa": nrm(ks[19], (N_EVEN, 2, LRU_BLOCKS, LRU_BLOCK, LRU_BLOCK), LRU_BLOCK ** -0.5),
        "lru_ba": nrm(ks[20], (N_EVEN, 2, LRU_WIDTH), 0.02),
        "lru_wx": nrm(ks[21], (N_EVEN, 2, LRU_BLOCKS, LRU_BLOCK, LRU_BLOCK), LRU_BLOCK ** -0.5),
        "lru_bx": nrm(ks[23], (N_EVEN, 2, LRU_WIDTH), 0.02),
        "lru_lambda": lru_lambda,
        "even_w_out": nrm(ks[24], (N_EVEN, HYENA_WIDTH + LRU_WIDTH, D_MODEL), (HYENA_WIDTH + LRU_WIDTH) ** -0.5),
        "attn_w_qkv": nrm(ks[25], (N_ODD, D_MODEL, 3 * D_MODEL), D_MODEL ** -0.5),
        "attn_lambda": nrm(ks[26], (N_ODD, 4, ATT_HEAD_DIM), 0.1),
        "attn_subln": gain(ks[27], (N_ODD, ATT_V_DIM)),
        "attn_w_o": nrm(ks[28], (N_ODD, D_MODEL, D_MODEL), D_MODEL ** -0.5),
    }


def reference(x_prompt, x_sample, ffn_norm, ffn_w1, ffn_w3, ffn_w2, mix_norm, final_norm,
              even_w_in, hyena_conv_w, hyena_conv_b, hyena_filt_w_in, hyena_filt_b, hyena_filt_w_hid,
              hyena_filt_freq, hyena_filt_w_out, hyena_bias, lru_conv_w, lru_conv_b, lru_wa, lru_ba,
              lru_wx, lru_bx, lru_lambda, even_w_out, attn_w_qkv, attn_lambda, attn_subln, attn_w_o):
    y_prompt = _trunk(x_prompt, ffn_norm, ffn_w1, ffn_w3, ffn_w2, mix_norm, final_norm,
                      even_w_in, hyena_conv_w, hyena_conv_b, hyena_filt_w_in, hyena_filt_b, hyena_filt_w_hid,
                      hyena_filt_freq, hyena_filt_w_out, hyena_bias, lru_conv_w, lru_conv_b, lru_wa, lru_ba,
                      lru_wx, lru_bx, lru_lambda, even_w_out, attn_w_qkv, attn_lambda, attn_subln, attn_w_o)
    y_sample = _trunk(x_sample, ffn_norm, ffn_w1, ffn_w3, ffn_w2, mix_norm, final_norm,
                      even_w_in, hyena_conv_w, hyena_conv_b, hyena_filt_w_in, hyena_filt_b, hyena_filt_w_hid,
                      hyena_filt_freq, hyena_filt_w_out, hyena_bias, lru_conv_w, lru_conv_b, lru_wa, lru_ba,
                      lru_wx, lru_bx, lru_lambda, even_w_out, attn_w_qkv, attn_lambda, attn_subln, attn_w_o)
    return (y_prompt, y_sample)
```

```python
import functools
import math

import numpy as np
import jax
import jax.numpy as jnp
from jax import lax
from jax.experimental import pallas as pl
from jax.experimental.pallas import tpu as pltpu

F32 = jnp.float32
BF16 = jnp.bfloat16

NORM_EPS = 1e-6
HYENA_SHORT = 3
HYENA_EMB = 33
HYENA_BANDS = (HYENA_EMB - 1) // 2
HYENA_INNER = 2
HYENA_TARGET = 1e-2
HYENA_FAST = 0.3
HYENA_SLOW = 1.5
LRU_BLOCKS = 8
LRU_CONV = 4
LRU_C = 8.0
ATT_HEADS = 16
ROPE_THETA = 10000.0
LOG2E = 1.4426950408889634

LANES = 128
SUBLANES = 8
VMEM_LIMIT_BYTES = 56 * 1024 * 1024
FFT_INNER = 128


def _params(*sem):
    return pltpu.CompilerParams(dimension_semantics=sem, vmem_limit_bytes=VMEM_LIMIT_BYTES)


def _rms(x, g):
    ms = jnp.mean(x * x, axis=-1, keepdims=True)
    return x * lax.rsqrt(ms + NORM_EPS) * g


def _ffn_kernel(x_ref, g_ref, w1_ref, w3_ref, w2_ref, o_ref, xn_ref, acc_ref):
    f = pl.program_id(1)

    @pl.when(f == 0)
    def _():
        xn_ref[...] = _rms(x_ref[...], g_ref[...]).astype(BF16)

    xn = xn_ref[...]
    a = jnp.dot(xn, w1_ref[...], preferred_element_type=F32)
    b = jnp.dot(xn, w3_ref[...], preferred_element_type=F32)
    h = (a * jax.nn.sigmoid(a) * b).astype(BF16)
    contrib = jnp.dot(h, w2_ref[...], preferred_element_type=F32)

    @pl.when(f == 0)
    def _():
        acc_ref[...] = contrib

    @pl.when(f > 0)
    def _():
        acc_ref[...] += contrib

    @pl.when(f == pl.num_programs(1) - 1)
    def _():
        o_ref[...] = x_ref[...] + 0.5 * acc_ref[...]


def _ffn(x, g, w1, w3, w2, *, tm=512, tf=512):
    m, d = x.shape
    ff = w1.shape[1]
    tm = min(tm, m)
    tf = min(tf, ff)
    return pl.pallas_call(
        _ffn_kernel,
        grid=(m // tm, ff // tf),
        in_specs=[
            pl.BlockSpec((tm, d), lambda i, f: (i, 0)),
            pl.BlockSpec((1, d), lambda i, f: (0, 0)),
            pl.BlockSpec((d, tf), lambda i, f: (0, f)),
            pl.BlockSpec((d, tf), lambda i, f: (0, f)),
            pl.BlockSpec((tf, d), lambda i, f: (f, 0)),
        ],
        out_specs=pl.BlockSpec((tm, d), lambda i, f: (i, 0)),
        out_shape=jax.ShapeDtypeStruct((m, d), F32),
        scratch_shapes=[pltpu.VMEM((tm, d), BF16), pltpu.VMEM((tm, d), F32)],
        compiler_params=_params("parallel", "arbitrary"),
        name="ffn",
    )(x, g.reshape(1, d), w1, w3, w2)


def _norm_matmul_kernel(x_ref, g_ref, w_ref, o_ref, xn_ref):
    @pl.when(pl.program_id(1) == 0)
    def _():
        xn_ref[...] = _rms(x_ref[...], g_ref[...]).astype(BF16)

    o_ref[...] = jnp.dot(xn_ref[...], w_ref[...], preferred_element_type=F32).astype(o_ref.dtype)


def _norm_matmul(x, g, w, *, out_dtype=F32, tm=512, tn=512):
    m, d = x.shape
    n = w.shape[1]
    tm = min(tm, m)
    tn = min(tn, n)
    return pl.pallas_call(
        _norm_matmul_kernel,
        grid=(m // tm, n // tn),
        in_specs=[
            pl.BlockSpec((tm, d), lambda i, j: (i, 0)),
            pl.BlockSpec((1, d), lambda i, j: (0, 0)),
            pl.BlockSpec((d, tn), lambda i, j: (0, j)),
        ],
        out_specs=pl.BlockSpec((tm, tn), lambda i, j: (i, j)),
        out_shape=jax.ShapeDtypeStruct((m, n), out_dtype),
        scratch_shapes=[pltpu.VMEM((tm, d), BF16)],
        compiler_params=_params("parallel", "arbitrary"),
        name="norm_matmul",
    )(x, g.reshape(1, d), w)


def _rope_tables(lmax, head_dim):
    half = head_dim // 2
    inv = ROPE_THETA ** (-np.arange(0, head_dim, 2, dtype=np.float64) / head_dim)
    ang = np.arange(lmax, dtype=np.float64)[:, None] * inv[None, :]
    ang = np.concatenate([ang, ang], axis=-1)
    cos = np.tile(np.cos(ang), (1, LANES // head_dim))
    sin = np.tile(np.sin(ang), (1, LANES // head_dim))
    first = (np.arange(LANES) % head_dim) < half
    sin_up = np.where(first[None, :], -sin, 0.0)
    sin_dn = np.where(first[None, :], 0.0, sin)
    return (jnp.asarray(cos, F32), jnp.asarray(sin_up, F32), jnp.asarray(sin_dn, F32))


def _qkv_kernel(x_ref, g_ref, w_ref, cos_ref, sup_ref, sdn_ref, o_ref, xn_ref, *, n_rot, half, q_scale):
    j = pl.program_id(1)

    @pl.when(j == 0)
    def _():
        xn_ref[...] = _rms(x_ref[...], g_ref[...]).astype(BF16)

    res = jnp.dot(xn_ref[...], w_ref[...], preferred_element_type=F32)
    tn = res.shape[1]

    def rope(scale):
        cos = cos_ref[...]
        sup = sup_ref[...]
        sdn = sdn_ref[...]
        for c in range(tn // LANES):
            blk = res[:, c * LANES:(c + 1) * LANES]
            rot = (blk * cos + pltpu.roll(blk, LANES - half, 1) * sup + pltpu.roll(blk, half, 1) * sdn)
            if scale != 1.0:
                rot = rot * scale
            o_ref[:, c * LANES:(c + 1) * LANES] = rot.astype(o_ref.dtype)

    @pl.when(j < n_rot)
    def _():
        rope(q_scale)

    @pl.when(jnp.logical_and(j >= n_rot, j < 2 * n_rot))
    def _():
        rope(1.0)

    @pl.when(j >= 2 * n_rot)
    def _():
        o_ref[...] = res.astype(o_ref.dtype)


def _qkv(x, g, w, tables, groups, *, head_dim, tm=512, tn=512):
    m, d = x.shape
    n = w.shape[1]
    tm = min(tm, min(l for _, _, l in groups))
    n_rot = d // tn
    cos, sup, sdn = tables

    def pos_block(i, j):
        blk = 0
        for row0, nseq, l in groups:
            inside = jnp.logical_and(i * tm >= row0, i * tm < row0 + nseq * l)
            blk = jnp.where(inside, ((i * tm - row0) % l) // tm, blk)
        return (blk, 0)

    q_scale = head_dim ** -0.5 * LOG2E
    return pl.pallas_call(
        functools.partial(_qkv_kernel, n_rot=n_rot, half=head_dim // 2, q_scale=q_scale),
        grid=(m // tm, n // tn),
        in_specs=[
            pl.BlockSpec((tm, d), lambda i, j: (i, 0)),
            pl.BlockSpec((1, d), lambda i, j: (0, 0)),
            pl.BlockSpec((d, tn), lambda i, j: (0, j)),
            pl.BlockSpec((tm, LANES), pos_block),
            pl.BlockSpec((tm, LANES), pos_block),
            pl.BlockSpec((tm, LANES), pos_block),
        ],
        out_specs=pl.BlockSpec((tm, tn), lambda i, j: (i, j)),
        out_shape=jax.ShapeDtypeStruct((m, n), BF16),
        scratch_shapes=[pltpu.VMEM((tm, d), BF16)],
        compiler_params=_params("parallel", "arbitrary"),
        name="qkv_rope",
    )(x, g.reshape(1, d), w, cos, sup, sdn)


def _attn_kernel(lam_ref, g_ref, q_ref, k_ref, v_ref, o_ref, vt_ref, acc_ref, st_ref, *, tk, lambda_init):
    seq_len = k_ref.shape[0]
    nchunk = seq_len // tk
    half = q_ref.shape[1] // 2

    @pl.when(pl.program_id(2) == 0)
    def _():
        for c in range(nchunk):
            vt_ref[c] = v_ref[c * tk:(c + 1) * tk, :].astype(F32).T.astype(BF16)

    q = q_ref[...]
    lane = lax.broadcasted_iota(jnp.int32, q.shape, 1)
    zero = jnp.zeros_like(q)
    qs = (jnp.where(lane < half, q, zero), jnp.where(lane >= half, q, zero))

    acc_ref[...] = jnp.zeros_like(acc_ref)
    st_ref[...] = jnp.where(lax.broadcasted_iota(jnp.int32, st_ref.shape, 0) % 2 == 0, -jnp.inf, 0.0)

    def body(c, carry):
        off = pl.multiple_of(c * tk, tk)
        k = k_ref[pl.ds(off, tk), :]
        vt = vt_ref[c]
        for comp in range(2):
            s = lax.dot_general(k, qs[comp], (((1,), (1,)), ((), ())), preferred_element_type=F32)
            m_old = st_ref[2 * comp:2 * comp + 1, :]
            m_new = jnp.maximum(m_old, jnp.max(s, axis=0, keepdims=True))
            alpha = jnp.exp2(m_old - m_new)
            p = jnp.exp2(s - m_new)
            st_ref[2 * comp + 1:2 * comp + 2, :] = (alpha * st_ref[2 * comp + 1:2 * comp + 2, :]
                                                    + jnp.sum(p, axis=0, keepdims=True))
            st_ref[2 * comp:2 * comp + 1, :] = m_new
            acc_ref[comp] = alpha * acc_ref[comp] + jnp.dot(vt, p.astype(BF16), preferred_element_type=F32)
        return carry

    lax.fori_loop(0, nchunk, body, 0)

    lv = lam_ref[...]
    lam = (jnp.exp(jnp.sum(lv[0:1] * lv[1:2], axis=1, keepdims=True))
           - jnp.exp(jnp.sum(lv[2:3] * lv[3:4], axis=1, keepdims=True)) + lambda_init)
    ot = acc_ref[0] * (1.0 / st_ref[1:2, :]) - acc_ref[1] * (lam * (1.0 / st_ref[3:4, :]))
    ms = jnp.mean(ot * ot, axis=0, keepdims=True)
    ot = ot * lax.rsqrt(ms + NORM_EPS) * g_ref[...] * (1.0 - lambda_init)
    o_ref[...] = ot.T.astype(o_ref.dtype)


def _diff_attention_group(qkv, lam_vec, subln_g, row0, nseq, seq_len, lambda_init, *, tq=256, tk=512):
    d = qkv.shape[1] // 3
    hw = d // ATT_HEADS
    tq = min(tq, seq_len)
    tk = min(tk, seq_len)
    nq = seq_len // tq
    qb0 = row0 // tq
    kb0 = row0 // seq_len
    return pl.pallas_call(
        functools.partial(_attn_kernel, tk=tk, lambda_init=lambda_init),
        grid=(nseq, ATT_HEADS, nq),
        in_specs=[
            pl.BlockSpec(lam_vec.shape, lambda b, h, i: (0, 0)),
            pl.BlockSpec((hw, 1), lambda b, h, i: (0, 0)),
            pl.BlockSpec((tq, hw), lambda b, h, i: (qb0 + b * nq + i, h)),
            pl.BlockSpec((seq_len, hw), lambda b, h, i: (kb0 + b, ATT_HEADS + h)),
            pl.BlockSpec((seq_len, hw), lambda b, h, i: (kb0 + b, 2 * ATT_HEADS + h)),
        ],
        out_specs=pl.BlockSpec((tq, hw), lambda b, h, i: (b * nq + i, h)),
        out_shape=jax.ShapeDtypeStruct((nseq * seq_len, d), BF16),
        scratch_shapes=[
            pltpu.VMEM((seq_len // tk, hw, tk), BF16),
            pltpu.VMEM((2, hw, tq), F32),
            pltpu.VMEM((4, tq), F32),
        ],
        compiler_params=_params("parallel", "parallel", "arbitrary"),
        name="diff_attention",
    )(lam_vec, subln_g.reshape(hw, 1), qkv, qkv, qkv)


def _proj_res_kernel(*refs, nparts):
    res_ref = refs[0]
    a_refs = refs[1:1 + nparts]
    w_refs = refs[1 + nparts:1 + 2 * nparts]
    o_ref = refs[1 + 2 * nparts]
    acc = res_ref[...]
    for a_ref, w_ref in zip(a_refs, w_refs):
        acc = acc + jnp.dot(a_ref[...].astype(BF16), w_ref[...], preferred_element_type=F32)
    o_ref[...] = acc


def _proj_residual(res, parts, w, *, tm=512, tn=512):
    m, n = res.shape
    tm = min(tm, m)
    tn = min(tn, n)
    nparts = len(parts)
    kp = w.shape[0] // nparts
    in_specs = [pl.BlockSpec((tm, tn), lambda i, j: (i, j))]
    in_specs += [pl.BlockSpec((tm, kp), lambda i, j: (i, 0)) for _ in parts]
    in_specs += [pl.BlockSpec((kp, tn), functools.partial(lambda i, j, p: (p, j), p=p)) for p in range(nparts)]
    return pl.pallas_call(
        functools.partial(_proj_res_kernel, nparts=nparts),
        grid=(m // tm, n // tn),
        in_specs=in_specs,
        out_specs=pl.BlockSpec((tm, tn), lambda i, j: (i, j)),
        out_shape=jax.ShapeDtypeStruct((m, n), F32),
        compiler_params=_params("parallel", "arbitrary"),
        name="proj_residual",
    )(res, *parts, *([w] * nparts))


def _final_norm_kernel(x_ref, g_ref, o_ref):
    o_ref[...] = _rms(x_ref[...], g_ref[...])


def _final_norm(x, g, *, tm=512):
    m, d = x.shape
    tm = min(tm, m)
    return pl.pallas_call(
        _final_norm_kernel,
        grid=(m // tm,),
        in_specs=[pl.BlockSpec((tm, d), lambda i: (i, 0)), pl.BlockSpec((1, d), lambda i: (0, 0))],
        out_specs=pl.BlockSpec((tm, d), lambda i: (i, 0)),
        out_shape=jax.ShapeDtypeStruct((m, d), F32),
        compiler_params=_params("parallel"),
        name="final_norm",
    )(x, g.reshape(1, d))


def _is_any(i, values):
    hit = i == values[0]
    for v in values[1:]:
        hit = jnp.logical_or(hit, i == v)
    return hit


def _hyena_pre_kernel(*refs, tt, first_tiles, last_tiles):
    (x0m, x0p, x0n, x1m, x1p, x1n, vm, vp, vn, w0, w1, w2, b0, b1, b2, x0_out, u_out) = refs
    i = pl.program_id(0)
    first = _is_any(i, first_tiles)
    last = _is_any(i, last_tiles)

    def conv(m_ref, p_ref, n_ref, w_ref, b_ref):
        x = m_ref[...]
        row = lax.broadcasted_iota(jnp.int32, x.shape, 0)
        prev_row = jnp.where(first, 0.0, p_ref[SUBLANES - 1:SUBLANES, :])
        next_row = jnp.where(last, 0.0, n_ref[0:1, :])
        xm = jnp.where(row == 0, prev_row, pltpu.roll(x, 1, 0))
        xp = jnp.where(row == tt - 1, next_row, pltpu.roll(x, tt - 1, 0))
        return b_ref[...] + xm * w_ref[0:1, :] + x * w_ref[1:2, :] + xp * w_ref[2:3, :]

    x0_out[...] = conv(x0m, x0p, x0n, w0, b0)
    u_out[...] = conv(vm, vp, vn, w2, b2) * conv(x1m, x1p, x1n, w1, b1)


def _hyena_pre(p, conv_w, conv_b, groups, width, *, tt=512, ct=512):
    m = p.shape[0]
    tt = min(tt, min(l for _, _, l in groups))
    ct = min(ct, width)
    nc = width // ct
    first_tiles = tuple((row0 + s * l) // tt for row0, nseq, l in groups for s in range(nseq))
    last_tiles = tuple((row0 + (s + 1) * l) // tt - 1 for row0, nseq, l in groups for s in range(nseq))
    hb = tt // SUBLANES
    nhb = m // SUBLANES

    def main(sec):
        return pl.BlockSpec((tt, ct), lambda i, j: (i, sec * nc + j))

    def prev(sec):
        return pl.BlockSpec((SUBLANES, ct), lambda i, j: (jnp.maximum(i * hb - 1, 0), sec * nc + j))

    def nxt(sec):
        return pl.BlockSpec((SUBLANES, ct), lambda i, j: (jnp.minimum((i + 1) * hb, nhb - 1), sec * nc + j))

    def wspec(sec):
        return pl.BlockSpec((HYENA_SHORT, ct), lambda i, j: (0, sec * nc + j))

    def bspec(sec):
        return pl.BlockSpec((1, ct), lambda i, j: (0, sec * nc + j))

    in_specs = []
    for sec in range(3):
        in_specs += [main(sec), prev(sec), nxt(sec)]
    in_specs += [wspec(0), wspec(1), wspec(2), bspec(0), bspec(1), bspec(2)]
    out_spec = pl.BlockSpec((tt, ct), lambda i, j: (i, j))
    cb = conv_b.reshape(1, -1)
    return pl.pallas_call(
        functools.partial(_hyena_pre_kernel, tt=tt, first_tiles=first_tiles, last_tiles=last_tiles),
        grid=(m // tt, nc),
        in_specs=in_specs,
        out_specs=[out_spec, out_spec],
        out_shape=[jax.ShapeDtypeStruct((m, width), F32)] * 2,
        compiler_params=_params("parallel", "arbitrary"),
        name="hyena_short_conv",
    )(*([p] * 9), conv_w, conv_w, conv_w, cb, cb, cb)


def _hyena_positions(seq_len):
    pos = np.arange(seq_len, dtype=np.float64)[:, None]
    t = pos / max(seq_len - 1, 1)
    w = (2.0 * math.pi / seq_len) * pos
    bands = np.linspace(1e-4, HYENA_BANDS - 1, HYENA_BANDS, dtype=np.float64)[None, :]
    z = np.concatenate([t, np.cos(bands * w), -np.sin(bands * w)], axis=-1)
    z2 = np.concatenate([z, z[:1], z[:0:-1]], axis=0)
    out = np.zeros((2 * seq_len, 64), np.float32)
    out[:, :HYENA_EMB] = z2
    return jnp.asarray(out)


def _filter_kernel(z_ref, win_ref, b_ref, whid_ref, fr_ref, wout_ref, dl_ref, o_ref, *, seq_len):
    tr = z_ref.shape[0]
    hp = lax.Precision.HIGHEST
    z = z_ref[...]
    fr = fr_ref[...]
    h = jnp.sin(fr * (jnp.dot(z, win_ref[...], precision=hp, preferred_element_type=F32) + b_ref[0:1, :]))
    for j in range(HYENA_INNER):
        h = jnp.sin(fr * (jnp.dot(h, whid_ref[j], precision=hp, preferred_element_type=F32)
                          + b_ref[j + 1:j + 2, :]))
    k = jnp.dot(h, wout_ref[...], precision=hp, preferred_element_type=F32)
    k = k * jnp.exp(-z[:, 0:1] * dl_ref[...])
    row = pl.program_id(0) * tr + lax.broadcasted_iota(jnp.int32, k.shape, 0)
    o_ref[...] = jnp.where(row == seq_len, 0.0, k)


def _hyena_filter(seq_len, w_in, b, w_hid, freq, w_out, *, tr=512):
    order = w_in.shape[1]
    width = w_out.shape[1] // 2
    tr = min(tr, seq_len)
    nfwd = seq_len // tr
    z2 = _hyena_positions(seq_len)
    w_in_p = jnp.zeros((z2.shape[1], order), F32).at[:HYENA_EMB].set(w_in)
    deltas = np.abs(np.linspace(math.log(HYENA_TARGET) / HYENA_SLOW, math.log(HYENA_TARGET) / HYENA_FAST,
                                width, dtype=np.float64))[None, :].astype(np.float32)
    return pl.pallas_call(
        functools.partial(_filter_kernel, seq_len=seq_len),
        grid=(2 * nfwd,),
        in_specs=[
            pl.BlockSpec((tr, z2.shape[1]), lambda i: (i, 0)),
            pl.BlockSpec(w_in_p.shape, lambda i: (0, 0)),
            pl.BlockSpec(b.shape, lambda i: (0, 0)),
            pl.BlockSpec(w_hid.shape, lambda i: (0, 0, 0)),
            pl.BlockSpec((1, order), lambda i: (0, 0)),
            pl.BlockSpec((order, width), lambda i: (0, i // nfwd)),
            pl.BlockSpec((1, width), lambda i: (0, 0)),
        ],
        out_specs=pl.BlockSpec((tr, width), lambda i: (i, 0)),
        out_shape=jax.ShapeDtypeStruct((2 * seq_len, width), F32),
        compiler_params=_params("parallel"),
        name="hyena_filter",
    )(z2, w_in_p, b, w_hid, freq.reshape(1, order), w_out, jnp.asarray(deltas))


def _dft_constants(seq_len):
    n = 2 * seq_len
    n2 = FFT_INNER
    n1 = n // n2
    a1 = 2.0 * np.pi * np.outer(np.arange(n1), np.arange(n1)) / n1
    f1 = np.concatenate([np.cos(a1), -np.sin(a1)], axis=0)
    f3 = np.concatenate([np.cos(a1), -np.sin(a1)], axis=1)[:n1 // 2]
    a2 = 2.0 * np.pi * np.outer(np.arange(n2), np.arange(n2)) / n2
    gr, gi = np.cos(a2), -np.sin(a2)
    g_fwd = np.block([[gr, -gi], [gi, gr]])
    g_inv = np.block([[gr, gi], [-gi, gr]])
    at = 2.0 * np.pi * np.outer(np.arange(n1), np.arange(n2)) / n
    tw_r = np.cos(at)[:, :, None]
    tw_i = -np.sin(at)[:, :, None]
    c = lambda a, dt: jnp.asarray(a, dt)
    return dict(n1=n1, n2=n2, f1_full=c(f1, BF16), f1_half=c(f1[:, :n1 // 2], BF16), f3=c(f3, BF16),
                g_fwd=c(g_fwd, BF16), g_inv=c(g_inv, BF16), tw_r=c(tw_r, F32), tw_i=c(tw_i, F32))


def _left_matmul_kernel(f_ref, x_ref, o_ref):
    o_ref[...] = jnp.dot(f_ref[...], x_ref[...].astype(BF16), preferred_element_type=F32)


def _left_matmul(fmat, x, *, tc=4096):
    nb, k, ncols = x.shape
    mo = fmat.shape[0]
    tc = min(tc, ncols)
    return pl.pallas_call(
        _left_matmul_kernel,
        grid=(nb, ncols // tc),
        in_specs=[
            pl.BlockSpec((mo, k), lambda b, j: (0, 0)),
            pl.BlockSpec((None, k, tc), lambda b, j: (b, 0, j)),
        ],
        out_specs=pl.BlockSpec((None, mo, tc), lambda b, j: (b, 0, j)),
        out_shape=jax.ShapeDtypeStruct((nb, mo, ncols), F32),
        compiler_params=_params("parallel", "parallel"),
        name="dft_outer_forward",
    )(fmat, x)


def _twiddle_dft(ar_ref, ai_ref, twr_ref, twi_ref, g_ref):
    n2 = ar_ref.shape[0]
    ar, ai = ar_ref[...], ai_ref[...]
    tr, ti = twr_ref[...], twi_ref[...]
    z = jnp.concatenate([ar * tr - ai * ti, ar * ti + ai * tr], axis=0).astype(BF16)
    x = jnp.dot(g_ref[...], z, preferred_element_type=F32)
    return x[:n2], x[n2:]


def _spectrum_kernel(ar_ref, ai_ref, twr_ref, twi_ref, g_ref, kr_ref, ki_ref, *, scale):
    xr, xi = _twiddle_dft(ar_ref, ai_ref, twr_ref, twi_ref, g_ref)
    kr_ref[...] = xr * scale
    ki_ref[...] = xi * scale


def _fft_mid_kernel(ar_ref, ai_ref, twr_ref, twi_ref, g_ref, gi_ref, kr_ref, ki_ref, br_ref, bi_ref):
    n2 = ar_ref.shape[0]
    xr, xi = _twiddle_dft(ar_ref, ai_ref, twr_ref, twi_ref, g_ref)
    kr, ki = kr_ref[...], ki_ref[...]
    y = jnp.concatenate([xr * kr - xi * ki, xr * ki + xi * kr], axis=0).astype(BF16)
    b = jnp.dot(gi_ref[...], y, preferred_element_type=F32)
    br, bi = b[:n2], b[n2:]
    tr, ti = twr_ref[...], twi_ref[...]
    br_ref[...] = br * tr + bi * ti
    bi_ref[...] = bi * tr - br * ti


def _fft_mid(a, consts, kf=None, *, ct=512):
    nb, _, n2, c = a.shape
    n1 = consts["n1"]
    ct = min(ct, c)
    grid = (nb, n1, c // ct)
    blk = lambda im: pl.BlockSpec((None, None, n2, ct), lambda b, k, j: (b, im * n1 + k, 0, j))
    tw = pl.BlockSpec((None, n2, 1), lambda b, k, j: (k, 0, 0))
    gs = pl.BlockSpec((2 * n2, 2 * n2), lambda b, k, j: (0, 0))
    half_spec = pl.BlockSpec((None, None, n2, ct), lambda b, k, j: (b, k, 0, j))
    half_shape = jax.ShapeDtypeStruct((nb, n1, n2, c), F32)
    if kf is None:
        return pl.pallas_call(
            functools.partial(_spectrum_kernel, scale=1.0 / (n1 * n2)),
            grid=grid,
            in_specs=[blk(0), blk(1), tw, tw, gs],
            out_specs=[half_spec, half_spec],
            out_shape=[half_shape, half_shape],
            compiler_params=_params("parallel", "parallel", "parallel"),
            name="dft_filter_spectrum",
        )(a, a, consts["tw_r"], consts["tw_i"], consts["g_fwd"])
    kblk = lambda: pl.BlockSpec((None, None, n2, ct), lambda b, k, j: (0, k, 0, j))
    return pl.pallas_call(
        _fft_mid_kernel,
        grid=grid,
        in_specs=[blk(0), blk(1), tw, tw, gs, gs, kblk(), kblk()],
        out_specs=[half_spec, half_spec],
        out_shape=[half_shape, half_shape],
        compiler_params=_params("parallel", "parallel", "parallel"),
        name="dft_inner_filter",
    )(a, a, consts["tw_r"], consts["tw_i"], consts["g_fwd"], consts["g_inv"], kf[0], kf[1])


def _dft_out_kernel(f_ref, br_ref, bi_ref, u_ref, x0_ref, bias_ref, o_ref):
    b = jnp.concatenate([br_ref[...], bi_ref[...]], axis=0).astype(BF16)
    y = jnp.dot(f_ref[...], b, preferred_element_type=F32)
    u = u_ref[...]
    o_ref[...] = x0_ref[...] * (y + u * bias_ref[...])


def _dft_out(f3, br, bi, u, x0, bias_cols, *, tc=4096):
    nb, n1, ncols = br.shape
    mo = f3.shape[0]
    tc = min(tc, ncols)
    dat = lambda rows: pl.BlockSpec((None, rows, tc), lambda b, j: (b, 0, j))
    return pl.pallas_call(
        _dft_out_kernel,
        grid=(nb, ncols // tc),
        in_specs=[pl.BlockSpec(f3.shape, lambda b, j: (0, 0)), dat(n1), dat(n1), dat(mo), dat(mo),
                  pl.BlockSpec((1, tc), lambda b, j: (0, j))],
        out_specs=dat(mo),
        out_shape=jax.ShapeDtypeStruct((nb, mo, ncols), F32),
        compiler_params=_params("parallel", "parallel"),
        name="dft_outer_inverse",
    )(f3, br, bi, u, x0, bias_cols)


def _hyena_long_conv_group(u, x0, k_two, bias, nseq, seq_len):
    c = u.shape[1]
    consts = _dft_constants(seq_len)
    n1, n2 = consts["n1"], consts["n2"]
    ak = _left_matmul(consts["f1_full"], k_two.reshape(1, n1, n2 * c))
    kfr, kfi = _fft_mid(ak.reshape(1, 2 * n1, n2, c), consts)
    ur = u.reshape(nseq, n1 // 2, n2 * c)
    a = _left_matmul(consts["f1_half"], ur)
    br, bi = _fft_mid(a.reshape(nseq, 2 * n1, n2, c), consts, kf=(kfr, kfi))
    y = _dft_out(consts["f3"], br.reshape(nseq, n1, n2 * c), bi.reshape(nseq, n1, n2 * c), ur,
                 x0.reshape(nseq, n1 // 2, n2 * c), jnp.tile(bias, n2).reshape(1, n2 * c))
    return y.reshape(nseq * seq_len, c)


def _softplus(z):
    return jnp.maximum(z, 0.0) + jnp.log(1.0 + jnp.exp(-jnp.abs(z)))


def _lru_kernel(pl_ref, pg_ref, cw_ref, cb_ref, wa_ref, ba_ref, wx_ref, bx_ref, lam_ref, o_ref, xp_ref, hs_ref,
                *, tchunk):
    seq_len, cw = pl_ref.shape
    nchunk = seq_len // tchunk
    halo = SUBLANES
    xp_ref[0:halo, :] = jnp.zeros((halo, cw), F32)
    xp_ref[halo + seq_len:2 * halo + seq_len, :] = jnp.zeros((halo, cw), F32)
    xp_ref[halo:halo + seq_len, :] = pl_ref[...]

    row = lax.broadcasted_iota(jnp.int32, (tchunk, cw), 0)
    nsteps = int(math.log2(tchunk))
    neg_c_softplus = -LRU_C * _softplus(-lam_ref[...])

    def gates(c, direction):
        base = pl.multiple_of(c * tchunk, tchunk)
        xw = xp_ref[pl.ds(base, tchunk + 2 * halo), :]
        ext = tchunk + 2 * halo
        xb = cb_ref[...]
        for j in range(LRU_CONV):
            shift = (LRU_CONV // 2 - j) % ext
            tap = xw if shift == 0 else pltpu.roll(xw, shift, 0)
            xb = xb + tap[halo:halo + tchunk, :] * cw_ref[j:j + 1, :]
        xb16 = xb.astype(BF16)
        r = jax.nn.sigmoid(jnp.dot(xb16, wa_ref[direction], preferred_element_type=F32)
                           + ba_ref[direction:direction + 1, :])
        i = jax.nn.sigmoid(jnp.dot(xb16, wx_ref[direction], preferred_element_type=F32)
                           + bx_ref[direction:direction + 1, :])
        log_a = r * neg_c_softplus[direction:direction + 1, :]
        a = jnp.exp(log_a)
        gain = jnp.sqrt(-jnp.tanh(log_a) * (a * a + 1.0))
        return a, gain * (i * xb)

    def scan(a, b, reverse):
        for s in range(nsteps):
            d = 1 << s
            if reverse:
                keep = row < tchunk - d
                a_sh = pltpu.roll(a, tchunk - d, 0)
                b_sh = pltpu.roll(b, tchunk - d, 0)
            else:
                keep = row >= d
                a_sh = pltpu.roll(a, d, 0)
                b_sh = pltpu.roll(b, d, 0)
            b = jnp.where(keep, a * b_sh + b, b)
            a = jnp.where(keep, a * a_sh, a)
        return a, b

    def fwd_body(c, h):
        a, b = gates(c, 0)
        a, b = scan(a, b, False)
        hc = a * h + b
        hs_ref[pl.ds(pl.multiple_of(c * tchunk, tchunk), tchunk), :] = hc
        return hc[tchunk - 1:tchunk, :]

    lax.fori_loop(0, nchunk, fwd_body, jnp.zeros((1, cw), F32))

    def rev_body(cc, h):
        c = nchunk - 1 - cc
        a, b = gates(c, 1)
        a, b = scan(a, b, True)
        hc = a * h + b
        sl = pl.ds(pl.multiple_of(c * tchunk, tchunk), tchunk)
        o_ref[sl, :] = (hs_ref[sl, :] + hc) * jax.nn.gelu(pg_ref[sl, :])
        return hc[0:1, :]

    lax.fori_loop(0, nchunk, rev_body, jnp.zeros((1, cw), F32))


def _rg_lru_group(p, col_gate, col_x, conv_w, conv_b, wa, ba, wx, bx, lam, row0, nseq, seq_len, *, tchunk=128):
    nblk = wa.shape[1]
    cw = wa.shape[2]
    tchunk = min(tchunk, seq_len)
    sb0 = row0 // seq_len
    gb0 = col_gate // cw
    xb0 = col_x // cw
    vec = lambda rows: pl.BlockSpec((rows, cw), lambda b, n: (0, n))
    mat = pl.BlockSpec((2, None, cw, cw), lambda b, n: (0, n, 0, 0))
    return pl.pallas_call(
        functools.partial(_lru_kernel, tchunk=tchunk),
        grid=(nseq, nblk),
        in_specs=[
            pl.BlockSpec((seq_len, cw), lambda b, n: (sb0 + b, xb0 + n)),
            pl.BlockSpec((seq_len, cw), lambda b, n: (sb0 + b, gb0 + n)),
            vec(LRU_CONV), vec(1), mat, vec(2), mat, vec(2), vec(2),
        ],
        out_specs=pl.BlockSpec((seq_len, cw), lambda b, n: (b, n)),
        out_shape=jax.ShapeDtypeStruct((nseq * seq_len, nblk * cw), F32),
        scratch_shapes=[pltpu.VMEM((seq_len + 2 * SUBLANES, cw), F32), pltpu.VMEM((seq_len, cw), F32)],
        compiler_params=_params("parallel", "parallel"),
        name="rg_lru",
    )(p, p, conv_w, conv_b.reshape(1, -1), wa, ba, wx, bx, lam)


def kernel(x_prompt, x_sample, ffn_norm, ffn_w1, ffn_w3, ffn_w2, mix_norm, final_norm, even_w_in, hyena_conv_w, hyena_conv_b, hyena_filt_w_in, hyena_filt_b, hyena_filt_w_hid, hyena_filt_freq, hyena_filt_w_out, hyena_bias, lru_conv_w, lru_conv_b, lru_wa, lru_ba, lru_wx, lru_bx, lru_lambda, even_w_out, attn_w_qkv, attn_lambda, attn_subln, attn_w_o):
    d = x_prompt.shape[-1]
    depth = ffn_norm.shape[0]
    groups = []
    row0 = 0
    for xg in (x_prompt, x_sample):
        groups.append((row0, xg.shape[0], xg.shape[1]))
        row0 += xg.shape[0] * xg.shape[1]
    groups = tuple(groups)
    x = jnp.concatenate([x_prompt.reshape(-1, d), x_sample.reshape(-1, d)], axis=0)

    hw = hyena_bias.shape[1]
    head_dim = d // (2 * ATT_HEADS)
    rope = _rope_tables(max(l for _, _, l in groups), head_dim)
    bf = lambda w: w.astype(BF16)

    for i in range(depth):
        j = i // 2
        x = _ffn(x, ffn_norm[i, 0], bf(ffn_w1[i, 0]), bf(ffn_w3[i, 0]), bf(ffn_w2[i, 0]))
        if i % 2 == 0:
            p = _norm_matmul(x, mix_norm[i], bf(even_w_in[j]))
            x0, u = _hyena_pre(p, hyena_conv_w[j], hyena_conv_b[j], groups, hw)
            wa, wx = bf(lru_wa[j]), bf(lru_wx[j])
            yh, yl = [], []
            for row0, nseq, seq_len in groups:
                rows = slice(row0, row0 + nseq * seq_len)
                k_two = _hyena_filter(seq_len, hyena_filt_w_in[j], hyena_filt_b[j], hyena_filt_w_hid[j],
                                      hyena_filt_freq[j], hyena_filt_w_out[j])
                yh.append(_hyena_long_conv_group(u[rows], x0[rows], k_two, hyena_bias[j], nseq, seq_len))
                yl.append(_rg_lru_group(p, 3 * hw, 3 * hw + wa.shape[1] * wa.shape[2], lru_conv_w[j],
                                        lru_conv_b[j], wa, lru_ba[j], wx, lru_bx[j], lru_lambda[j],
                                        row0, nseq, seq_len))
            x = _proj_residual(x, [jnp.concatenate(yh, axis=0), jnp.concatenate(yl, axis=0)], bf(even_w_out[j]))
        else:
            lambda_init = 0.8 - 0.6 * math.exp(-0.3 * i)
            qkv = _qkv(x, mix_norm[i], bf(attn_w_qkv[j]), rope, groups, head_dim=head_dim)
            o = [_diff_attention_group(qkv, attn_lambda[j], attn_subln[j], row0, nseq, seq_len, lambda_init)
                 for row0, nseq, seq_len in groups]
            x = _proj_residual(x, [jnp.concatenate(o, axis=0)], bf(attn_w_o[j]))
        x = _ffn(x, ffn_norm[i, 1], bf(ffn_w1[i, 1]), bf(ffn_w3[i, 1]), bf(ffn_w2[i, 1]))

    y = _final_norm(x, final_norm)
    outs = []
    for (row0, nseq, seq_len), xg in zip(groups, (x_prompt, x_sample)):
        outs.append(y[row0:row0 + nseq * seq_len].reshape(xg.shape))
    return tuple(outs)
```

```python
import functools
import math

import numpy as np
import jax
import jax.numpy as jnp
from jax import lax
from jax.experimental import pallas as pl
from jax.experimental.pallas import tpu as pltpu

F32 = jnp.float32
BF16 = jnp.bfloat16

NORM_EPS = 1e-6
HYENA_SHORT = 3
HYENA_EMB = 33
HYENA_BANDS = (HYENA_EMB - 1) // 2
HYENA_INNER = 2
HYENA_TARGET = 1e-2
HYENA_FAST = 0.3
HYENA_SLOW = 1.5
LRU_BLOCKS = 8
LRU_CONV = 4
LRU_C = 8.0
ATT_HEADS = 16
ROPE_THETA = 10000.0
LOG2E = 1.4426950408889634

LANES = 128
SUBLANES = 8
VMEM_LIMIT_BYTES = 56 * 1024 * 1024
FFT_INNER = 128


def _params(*sem):
    return pltpu.CompilerParams(dimension_semantics=sem, vmem_limit_bytes=VMEM_LIMIT_BYTES)


def _rms(x, g):
    ms = jnp.mean(x * x, axis=-1, keepdims=True)
    return x * lax.rsqrt(ms + NORM_EPS) * g


def _ffn_kernel(x_ref, g_ref, w1_ref, w3_ref, w2_ref, o_ref, xn_ref, acc_ref):
    f = pl.program_id(1)

    @pl.when(f == 0)
    def _():
        xn_ref[...] = _rms(x_ref[...], g_ref[...]).astype(BF16)
        acc_ref[...] = jnp.zeros_like(acc_ref)

    xn = xn_ref[...]
    a = jnp.dot(xn, w1_ref[...], preferred_element_type=F32)
    b = jnp.dot(xn, w3_ref[...], preferred_element_type=F32)
    h = (a * jax.nn.sigmoid(a) * b).astype(BF16)
    acc_ref[...] += jnp.dot(h, w2_ref[...], preferred_element_type=F32)

    @pl.when(f == pl.num_programs(1) - 1)
    def _():
        o_ref[...] = x_ref[...] + 0.5 * acc_ref[...]


def _ffn(x, g, w1, w3, w2, *, tm=512, tf=512):
    m, d = x.shape
    ff = w1.shape[1]
    tm = min(tm, m)
    tf = min(tf, ff)
    return pl.pallas_call(
        _ffn_kernel,
        grid=(m // tm, ff // tf),
        in_specs=[
            pl.BlockSpec((tm, d), lambda i, f: (i, 0)),
            pl.BlockSpec((1, d), lambda i, f: (0, 0)),
            pl.BlockSpec((d, tf), lambda i, f: (0, f)),
            pl.BlockSpec((d, tf), lambda i, f: (0, f)),
            pl.BlockSpec((tf, d), lambda i, f: (f, 0)),
        ],
        out_specs=pl.BlockSpec((tm, d), lambda i, f: (i, 0)),
        out_shape=jax.ShapeDtypeStruct((m, d), F32),
        scratch_shapes=[pltpu.VMEM((tm, d), BF16), pltpu.VMEM((tm, d), F32)],
        compiler_params=_params("parallel", "arbitrary"),
        name="ffn",
    )(x, g.reshape(1, d), w1, w3, w2)


def _norm_matmul_kernel(x_ref, g_ref, w_ref, o_ref, xn_ref):
    @pl.when(pl.program_id(1) == 0)
    def _():
        xn_ref[...] = _rms(x_ref[...], g_ref[...]).astype(BF16)

    o_ref[...] = jnp.dot(xn_ref[...], w_ref[...], preferred_element_type=F32).astype(o_ref.dtype)


def _norm_matmul(x, g, w, *, out_dtype=F32, tm=512, tn=512):
    m, d = x.shape
    n = w.shape[1]
    tm = min(tm, m)
    tn = min(tn, n)
    return pl.pallas_call(
        _norm_matmul_kernel,
        grid=(m // tm, n // tn),
        in_specs=[
            pl.BlockSpec((tm, d), lambda i, j: (i, 0)),
            pl.BlockSpec((1, d), lambda i, j: (0, 0)),
            pl.BlockSpec((d, tn), lambda i, j: (0, j)),
        ],
        out_specs=pl.BlockSpec((tm, tn), lambda i, j: (i, j)),
        out_shape=jax.ShapeDtypeStruct((m, n), out_dtype),
        scratch_shapes=[pltpu.VMEM((tm, d), BF16)],
        compiler_params=_params("parallel", "arbitrary"),
        name="norm_matmul",
    )(x, g.reshape(1, d), w)


def _rope_tables(lmax, head_dim):
    half = head_dim // 2
    inv = ROPE_THETA ** (-np.arange(0, head_dim, 2, dtype=np.float64) / head_dim)
    ang = np.arange(lmax, dtype=np.float64)[:, None] * inv[None, :]
    ang = np.concatenate([ang, ang], axis=-1)
    cos = np.tile(np.cos(ang), (1, LANES // head_dim))
    sin = np.tile(np.sin(ang), (1, LANES // head_dim))
    first = (np.arange(LANES) % head_dim) < half
    sin_up = np.where(first[None, :], -sin, 0.0)
    sin_dn = np.where(first[None, :], 0.0, sin)
    return (jnp.asarray(cos, F32), jnp.asarray(sin_up, F32), jnp.asarray(sin_dn, F32))


def _qkv_kernel(x_ref, g_ref, w_ref, cos_ref, sup_ref, sdn_ref, o_ref, xn_ref, *, n_rot, half, q_scale):
    j = pl.program_id(1)

    @pl.when(j == 0)
    def _():
        xn_ref[...] = _rms(x_ref[...], g_ref[...]).astype(BF16)

    res = jnp.dot(xn_ref[...], w_ref[...], preferred_element_type=F32)
    tn = res.shape[1]

    def rope(scale):
        cos = cos_ref[...]
        sup = sup_ref[...]
        sdn = sdn_ref[...]
        for c in range(tn // LANES):
            blk = res[:, c * LANES:(c + 1) * LANES]
            rot = (blk * cos + pltpu.roll(blk, LANES - half, 1) * sup + pltpu.roll(blk, half, 1) * sdn)
            if scale != 1.0:
                rot = rot * scale
            o_ref[:, c * LANES:(c + 1) * LANES] = rot.astype(o_ref.dtype)

    @pl.when(j < n_rot)
    def _():
        rope(q_scale)

    @pl.when(jnp.logical_and(j >= n_rot, j < 2 * n_rot))
    def _():
        rope(1.0)

    @pl.when(j >= 2 * n_rot)
    def _():
        o_ref[...] = res.astype(o_ref.dtype)


def _qkv(x, g, w, tables, groups, *, head_dim, tm=512, tn=512):
    m, d = x.shape
    n = w.shape[1]
    tm = min(tm, min(l for _, _, l in groups))
    n_rot = d // tn
    cos, sup, sdn = tables

    def pos_block(i, j):
        blk = 0
        for row0, nseq, l in groups:
            inside = jnp.logical_and(i * tm >= row0, i * tm < row0 + nseq * l)
            blk = jnp.where(inside, ((i * tm - row0) % l) // tm, blk)
        return (blk, 0)

    q_scale = head_dim ** -0.5 * LOG2E
    return pl.pallas_call(
        functools.partial(_qkv_kernel, n_rot=n_rot, half=head_dim // 2, q_scale=q_scale),
        grid=(m // tm, n // tn),
        in_specs=[
            pl.BlockSpec((tm, d), lambda i, j: (i, 0)),
            pl.BlockSpec((1, d), lambda i, j: (0, 0)),
            pl.BlockSpec((d, tn), lambda i, j: (0, j)),
            pl.BlockSpec((tm, LANES), pos_block),
            pl.BlockSpec((tm, LANES), pos_block),
            pl.BlockSpec((tm, LANES), pos_block),
        ],
        out_specs=pl.BlockSpec((tm, tn), lambda i, j: (i, j)),
        out_shape=jax.ShapeDtypeStruct((m, n), BF16),
        scratch_shapes=[pltpu.VMEM((tm, d), BF16)],
        compiler_params=_params("parallel", "arbitrary"),
        name="qkv_rope",
    )(x, g.reshape(1, d), w, cos, sup, sdn)


def _attn_kernel(lam_ref, g_ref, q_ref, k_ref, v_ref, *rest, tk, lambda_init):
    o_ref, vt_ref, acc_ref, s_ref = rest[-4:]
    seq_len = k_ref.shape[0]
    nchunk = seq_len // tk
    half = q_ref.shape[1] // 2

    @pl.when(pl.program_id(2) == 0)
    def _():
        for c in range(nchunk):
            vt_ref[c] = v_ref[c * tk:(c + 1) * tk, :].astype(F32).T.astype(BF16)

    q = q_ref[...]
    lane = lax.broadcasted_iota(jnp.int32, q.shape, 1)
    zero = jnp.zeros_like(q)
    qs = (jnp.where(lane < half, q, zero), jnp.where(lane >= half, q, zero))

    tq = q.shape[0]

    def score_body(c, ms):
        off = pl.multiple_of(c * tk, tk)
        k = k_ref[pl.ds(off, tk), :]
        new = []
        for comp in range(2):
            s = lax.dot_general(k, qs[comp], (((1,), (1,)), ((), ())), preferred_element_type=F32)
            s_ref[comp, c] = s
            new.append(jnp.maximum(ms[comp], jnp.max(s, axis=0, keepdims=True)))
        return tuple(new)

    neg = jnp.full((1, tq), -jnp.inf, F32)
    ms = lax.fori_loop(0, nchunk, score_body, (neg, neg), unroll=4)

    acc_ref[...] = jnp.zeros_like(acc_ref)

    def pv_body(c, ls):
        vt = vt_ref[c]
        new = []
        for comp in range(2):
            p = jnp.exp2(s_ref[comp, c] - ms[comp])
            new.append(ls[comp] + jnp.sum(p, axis=0, keepdims=True))
            acc_ref[comp] += jnp.dot(vt, p.astype(BF16), preferred_element_type=F32)
        return tuple(new)

    zero_row = jnp.zeros((1, tq), F32)
    ls = lax.fori_loop(0, nchunk, pv_body, (zero_row, zero_row), unroll=4)

    lv = lam_ref[...]
    lam = (jnp.exp(jnp.sum(lv[0:1] * lv[1:2], axis=1, keepdims=True))
           - jnp.exp(jnp.sum(lv[2:3] * lv[3:4], axis=1, keepdims=True)) + lambda_init)
    ot = acc_ref[0] * (1.0 / ls[0]) - acc_ref[1] * (lam * (1.0 / ls[1]))
    ms = jnp.mean(ot * ot, axis=0, keepdims=True)
    ot = ot * lax.rsqrt(ms + NORM_EPS) * g_ref[...] * (1.0 - lambda_init)
    o_ref[...] = ot.T.astype(o_ref.dtype)


def _alias_prev(in_specs, args, prev):
    if prev is None:
        return {}
    in_specs.append(pl.BlockSpec(memory_space=pl.ANY))
    args.append(prev)
    return {len(args) - 1: 0}


def _diff_attention_group(qkv, lam_vec, subln_g, row0, nseq, seq_len, lambda_init, prev, *, tq=256, tk=512):
    m = qkv.shape[0]
    d = qkv.shape[1] // 3
    hw = d // ATT_HEADS
    tq = min(tq, seq_len)
    tk = min(tk, seq_len)
    nq = seq_len // tq
    qb0 = row0 // tq
    kb0 = row0 // seq_len
    in_specs = [
        pl.BlockSpec(lam_vec.shape, lambda b, h, i: (0, 0)),
        pl.BlockSpec((hw, 1), lambda b, h, i: (0, 0)),
        pl.BlockSpec((tq, hw), lambda b, h, i: (qb0 + b * nq + i, h)),
        pl.BlockSpec((seq_len, hw), lambda b, h, i: (kb0 + b, ATT_HEADS + h)),
        pl.BlockSpec((seq_len, hw), lambda b, h, i: (kb0 + b, 2 * ATT_HEADS + h)),
    ]
    args = [lam_vec, subln_g.reshape(hw, 1), qkv, qkv, qkv]
    aliases = _alias_prev(in_specs, args, prev)
    return pl.pallas_call(
        functools.partial(_attn_kernel, tk=tk, lambda_init=lambda_init),
        grid=(nseq, ATT_HEADS, nq),
        in_specs=in_specs,
        out_specs=pl.BlockSpec((tq, hw), lambda b, h, i: (qb0 + b * nq + i, h)),
        out_shape=jax.ShapeDtypeStruct((m, d), BF16),
        input_output_aliases=aliases,
        scratch_shapes=[
            pltpu.VMEM((seq_len // tk, hw, tk), BF16),
            pltpu.VMEM((2, hw, tq), F32),
            pltpu.VMEM((2, seq_len // tk, tk, tq), F32),
        ],
        compiler_params=_params("parallel", "parallel", "arbitrary"),
        name="diff_attention",
    )(*args)


def _proj_res_kernel(*refs, nparts):
    res_ref = refs[0]
    a_refs = refs[1:1 + nparts]
    w_refs = refs[1 + nparts:1 + 2 * nparts]
    o_ref = refs[1 + 2 * nparts]
    acc = res_ref[...]
    for a_ref, w_ref in zip(a_refs, w_refs):
        acc = acc + jnp.dot(a_ref[...].astype(BF16), w_ref[...], preferred_element_type=F32)
    o_ref[...] = acc


def _proj_residual(res, parts, w, *, tm=512, tn=2048):
    m, n = res.shape
    tm = min(tm, m)
    tn = min(tn, n)
    nparts = len(parts)
    kp = w.shape[0] // nparts
    in_specs = [pl.BlockSpec((tm, tn), lambda i, j: (i, j))]
    in_specs += [pl.BlockSpec((tm, kp), lambda i, j: (i, 0)) for _ in parts]
    in_specs += [pl.BlockSpec((kp, tn), functools.partial(lambda i, j, p: (p, j), p=p)) for p in range(nparts)]
    return pl.pallas_call(
        functools.partial(_proj_res_kernel, nparts=nparts),
        grid=(m // tm, n // tn),
        in_specs=in_specs,
        out_specs=pl.BlockSpec((tm, tn), lambda i, j: (i, j)),
        out_shape=jax.ShapeDtypeStruct((m, n), F32),
        compiler_params=_params("parallel", "arbitrary"),
        name="proj_residual",
    )(res, *parts, *([w] * nparts))


def _final_norm_kernel(x_ref, g_ref, o_ref):
    o_ref[...] = _rms(x_ref[...], g_ref[...])


def _final_norm(x, g, *, tm=512):
    m, d = x.shape
    tm = min(tm, m)
    return pl.pallas_call(
        _final_norm_kernel,
        grid=(m // tm,),
        in_specs=[pl.BlockSpec((tm, d), lambda i: (i, 0)), pl.BlockSpec((1, d), lambda i: (0, 0))],
        out_specs=pl.BlockSpec((tm, d), lambda i: (i, 0)),
        out_shape=jax.ShapeDtypeStruct((m, d), F32),
        compiler_params=_params("parallel"),
        name="final_norm",
    )(x, g.reshape(1, d))


def _is_any(i, values):
    hit = i == values[0]
    for v in values[1:]:
        hit = jnp.logical_or(hit, i == v)
    return hit


def _hyena_pre_kernel(*refs, tt, first_tiles, last_tiles):
    (x0m, x0p, x0n, x1m, x1p, x1n, vm, vp, vn, w0, w1, w2, b0, b1, b2, x0_out, u_out) = refs
    i = pl.program_id(0)
    first = _is_any(i, first_tiles)
    last = _is_any(i, last_tiles)

    def conv(m_ref, p_ref, n_ref, w_ref, b_ref):
        x = m_ref[...]
        row = lax.broadcasted_iota(jnp.int32, x.shape, 0)
        prev_row = jnp.where(first, 0.0, p_ref[SUBLANES - 1:SUBLANES, :])
        next_row = jnp.where(last, 0.0, n_ref[0:1, :])
        xm = jnp.where(row == 0, prev_row, pltpu.roll(x, 1, 0))
        xp = jnp.where(row == tt - 1, next_row, pltpu.roll(x, tt - 1, 0))
        return b_ref[...] + xm * w_ref[0:1, :] + x * w_ref[1:2, :] + xp * w_ref[2:3, :]

    x0_out[...] = conv(x0m, x0p, x0n, w0, b0)
    u_out[...] = conv(vm, vp, vn, w2, b2) * conv(x1m, x1p, x1n, w1, b1)


def _hyena_pre(p, conv_w, conv_b, groups, width, *, tt=512, ct=512):
    m = p.shape[0]
    tt = min(tt, min(l for _, _, l in groups))
    ct = min(ct, width)
    nc = width // ct
    first_tiles = tuple((row0 + s * l) // tt for row0, nseq, l in groups for s in range(nseq))
    last_tiles = tuple((row0 + (s + 1) * l) // tt - 1 for row0, nseq, l in groups for s in range(nseq))
    hb = tt // SUBLANES
    nhb = m // SUBLANES

    def main(sec):
        return pl.BlockSpec((tt, ct), lambda i, j: (i, sec * nc + j))

    def prev(sec):
        return pl.BlockSpec((SUBLANES, ct), lambda i, j: (jnp.maximum(i * hb - 1, 0), sec * nc + j))

    def nxt(sec):
        return pl.BlockSpec((SUBLANES, ct), lambda i, j: (jnp.minimum((i + 1) * hb, nhb - 1), sec * nc + j))

    def wspec(sec):
        return pl.BlockSpec((HYENA_SHORT, ct), lambda i, j: (0, sec * nc + j))

    def bspec(sec):
        return pl.BlockSpec((1, ct), lambda i, j: (0, sec * nc + j))

    in_specs = []
    for sec in range(3):
        in_specs += [main(sec), prev(sec), nxt(sec)]
    in_specs += [wspec(0), wspec(1), wspec(2), bspec(0), bspec(1), bspec(2)]
    out_spec = pl.BlockSpec((tt, ct), lambda i, j: (i, j))
    cb = conv_b.reshape(1, -1)
    return pl.pallas_call(
        functools.partial(_hyena_pre_kernel, tt=tt, first_tiles=first_tiles, last_tiles=last_tiles),
        grid=(m // tt, nc),
        in_specs=in_specs,
        out_specs=[out_spec, out_spec],
        out_shape=[jax.ShapeDtypeStruct((m, width), F32)] * 2,
        compiler_params=_params("parallel", "arbitrary"),
        name="hyena_short_conv",
    )(*([p] * 9), conv_w, conv_w, conv_w, cb, cb, cb)


def _hyena_positions(seq_len):
    pos = np.arange(seq_len, dtype=np.float64)[:, None]
    t = pos / max(seq_len - 1, 1)
    w = (2.0 * math.pi / seq_len) * pos
    bands = np.linspace(1e-4, HYENA_BANDS - 1, HYENA_BANDS, dtype=np.float64)[None, :]
    z = np.concatenate([t, np.cos(bands * w), -np.sin(bands * w)], axis=-1)
    z2 = np.concatenate([z, z[:1], z[:0:-1]], axis=0)
    out = np.zeros((2 * seq_len, 64), np.float32)
    out[:, :HYENA_EMB] = z2
    return jnp.asarray(out)


def _filter_kernel(z_ref, win_ref, b_ref, whid_ref, fr_ref, wout_ref, dl_ref, o_ref, *, seq_len):
    tr = z_ref.shape[0]
    hp = lax.Precision.HIGHEST
    z = z_ref[...]
    fr = fr_ref[...]
    h = jnp.sin(fr * (jnp.dot(z, win_ref[...], precision=hp, preferred_element_type=F32) + b_ref[0:1, :]))
    for j in range(HYENA_INNER):
        h = jnp.sin(fr * (jnp.dot(h, whid_ref[j], precision=hp, preferred_element_type=F32)
                          + b_ref[j + 1:j + 2, :]))
    k = jnp.dot(h, wout_ref[...], precision=hp, preferred_element_type=F32)
    k = k * jnp.exp(-z[:, 0:1] * dl_ref[...])
    row = pl.program_id(0) * tr + lax.broadcasted_iota(jnp.int32, k.shape, 0)
    o_ref[...] = jnp.where(row == seq_len, 0.0, k)


def _hyena_filter(seq_len, w_in, b, w_hid, freq, w_out, *, tr=512):
    order = w_in.shape[1]
    width = w_out.shape[1] // 2
    tr = min(tr, seq_len)
    nfwd = seq_len // tr
    z2 = _hyena_positions(seq_len)
    w_in_p = jnp.zeros((z2.shape[1], order), F32).at[:HYENA_EMB].set(w_in)
    deltas = np.abs(np.linspace(math.log(HYENA_TARGET) / HYENA_SLOW, math.log(HYENA_TARGET) / HYENA_FAST,
                                width, dtype=np.float64))[None, :].astype(np.float32)
    return pl.pallas_call(
        functools.partial(_filter_kernel, seq_len=seq_len),
        grid=(2 * nfwd,),
        in_specs=[
            pl.BlockSpec((tr, z2.shape[1]), lambda i: (i, 0)),
            pl.BlockSpec(w_in_p.shape, lambda i: (0, 0)),
            pl.BlockSpec(b.shape, lambda i: (0, 0)),
            pl.BlockSpec(w_hid.shape, lambda i: (0, 0, 0)),
            pl.BlockSpec((1, order), lambda i: (0, 0)),
            pl.BlockSpec((order, width), lambda i: (0, i // nfwd)),
            pl.BlockSpec((1, width), lambda i: (0, 0)),
        ],
        out_specs=pl.BlockSpec((tr, width), lambda i: (i, 0)),
        out_shape=jax.ShapeDtypeStruct((2 * seq_len, width), F32),
        compiler_params=_params("parallel"),
        name="hyena_filter",
    )(z2, w_in_p, b, w_hid, freq.reshape(1, order), w_out, jnp.asarray(deltas))


def _dft_constants(seq_len):
    n = 2 * seq_len
    n2 = FFT_INNER
    n1 = n // n2
    a1 = 2.0 * np.pi * np.outer(np.arange(n1), np.arange(n1)) / n1
    f1 = np.concatenate([np.cos(a1), -np.sin(a1)], axis=0)
    f3 = np.concatenate([np.cos(a1), -np.sin(a1)], axis=1)[:n1 // 2]
    a2 = 2.0 * np.pi * np.outer(np.arange(n2), np.arange(n2)) / n2
    gr, gi = np.cos(a2), -np.sin(a2)
    g_fwd = np.block([[gr, -gi], [gi, gr]])
    g_inv = np.block([[gr, gi], [-gi, gr]])
    at = 2.0 * np.pi * np.outer(np.arange(n1), np.arange(n2)) / n
    tw_r = np.cos(at)[:, :, None]
    tw_i = -np.sin(at)[:, :, None]
    c = lambda a, dt: jnp.asarray(a, dt)
    return dict(n1=n1, n2=n2, f1_full=c(f1, BF16), f1_half=c(f1[:, :n1 // 2], BF16), f3=c(f3, BF16),
                g_fwd=c(g_fwd, BF16), g_inv=c(g_inv, BF16), tw_r=c(tw_r, F32), tw_i=c(tw_i, F32))


def _dft_outer_fwd_kernel(f_ref, x_ref, o_ref):
    f = f_ref[...]
    for s in range(x_ref.shape[1]):
        o_ref[:, s, :] = jnp.dot(f, x_ref[:, s, :].astype(BF16), preferred_element_type=F32)


def _dft_outer_forward(fmat, x, seq0, nseq, *, ct=1024):
    _, k, n2, c = x.shape
    mo = fmat.shape[0]
    ct = min(ct, c)
    return pl.pallas_call(
        _dft_outer_fwd_kernel,
        grid=(nseq, n2 // SUBLANES, c // ct),
        in_specs=[
            pl.BlockSpec((mo, k), lambda b, s, j: (0, 0)),
            pl.BlockSpec((None, k, SUBLANES, ct), lambda b, s, j: (seq0 + b, 0, s, j)),
        ],
        out_specs=pl.BlockSpec((None, mo, SUBLANES, ct), lambda b, s, j: (b, 0, s, j)),
        out_shape=jax.ShapeDtypeStruct((nseq, mo, n2, c), F32),
        compiler_params=_params("parallel", "parallel", "parallel"),
        name="dft_outer_forward",
    )(fmat, x)


def _twiddle_dft(ar_ref, ai_ref, twr_ref, twi_ref, g_ref):
    n2 = ar_ref.shape[0]
    ar, ai = ar_ref[...], ai_ref[...]
    tr, ti = twr_ref[...], twi_ref[...]
    z = jnp.concatenate([ar * tr - ai * ti, ar * ti + ai * tr], axis=0).astype(BF16)
    x = jnp.dot(g_ref[...], z, preferred_element_type=F32)
    return x[:n2], x[n2:]


def _spectrum_kernel(ar_ref, ai_ref, twr_ref, twi_ref, g_ref, kr_ref, ki_ref, *, scale):
    xr, xi = _twiddle_dft(ar_ref, ai_ref, twr_ref, twi_ref, g_ref)
    kr_ref[...] = xr * scale
    ki_ref[...] = xi * scale


def _fft_mid_kernel(ar_ref, ai_ref, twr_ref, twi_ref, g_ref, gi_ref, kr_ref, ki_ref, br_ref, bi_ref):
    n2 = ar_ref.shape[0]
    xr, xi = _twiddle_dft(ar_ref, ai_ref, twr_ref, twi_ref, g_ref)
    kr, ki = kr_ref[...], ki_ref[...]
    y = jnp.concatenate([xr * kr - xi * ki, xr * ki + xi * kr], axis=0).astype(BF16)
    b = jnp.dot(gi_ref[...], y, preferred_element_type=F32)
    br, bi = b[:n2], b[n2:]
    tr, ti = twr_ref[...], twi_ref[...]
    br_ref[...] = br * tr + bi * ti
    bi_ref[...] = bi * tr - br * ti


def _fft_mid(a, consts, kf=None, *, ct=1024):
    nb, _, n2, c = a.shape
    n1 = consts["n1"]
    ct = min(ct, c)
    grid = (nb, n1, c // ct)
    blk = lambda im: pl.BlockSpec((None, None, n2, ct), lambda b, k, j: (b, im * n1 + k, 0, j))
    tw = pl.BlockSpec((None, n2, 1), lambda b, k, j: (k, 0, 0))
    gs = pl.BlockSpec((2 * n2, 2 * n2), lambda b, k, j: (0, 0))
    half_spec = pl.BlockSpec((None, None, n2, ct), lambda b, k, j: (b, k, 0, j))
    half_shape = jax.ShapeDtypeStruct((nb, n1, n2, c), F32)
    if kf is None:
        return pl.pallas_call(
            functools.partial(_spectrum_kernel, scale=1.0 / (n1 * n2)),
            grid=grid,
            in_specs=[blk(0), blk(1), tw, tw, gs],
            out_specs=[half_spec, half_spec],
            out_shape=[half_shape, half_shape],
            compiler_params=_params("parallel", "parallel", "parallel"),
            name="dft_filter_spectrum",
        )(a, a, consts["tw_r"], consts["tw_i"], consts["g_fwd"])
    kblk = lambda: pl.BlockSpec((None, None, n2, ct), lambda b, k, j: (0, k, 0, j))
    return pl.pallas_call(
        _fft_mid_kernel,
        grid=grid,
        in_specs=[blk(0), blk(1), tw, tw, gs, gs, kblk(), kblk()],
        out_specs=[half_spec, half_spec],
        out_shape=[half_shape, half_shape],
        compiler_params=_params("parallel", "parallel", "parallel"),
        name="dft_inner_filter",
    )(a, a, consts["tw_r"], consts["tw_i"], consts["g_fwd"], consts["g_inv"], kf[0], kf[1])


def _dft_outer_inv_kernel(f_ref, br_ref, bi_ref, u_ref, x0_ref, bias_ref, *rest):
    o_ref = rest[-1]
    f = f_ref[...]
    bias = bias_ref[...]
    for s in range(u_ref.shape[1]):
        b = jnp.concatenate([br_ref[:, s, :], bi_ref[:, s, :]], axis=0).astype(BF16)
        y = jnp.dot(f, b, preferred_element_type=F32)
        o_ref[:, s, :] = x0_ref[:, s, :] * (y + u_ref[:, s, :] * bias)


def _dft_outer_inverse(f3, br, bi, u, x0, bias, seq0, prev, *, ct=1024):
    nseq, n1, n2, c = br.shape
    mo = f3.shape[0]
    ct = min(ct, c)
    spec_in = pl.BlockSpec((None, n1, SUBLANES, ct), lambda b, s, j: (b, 0, s, j))
    spec_tok = pl.BlockSpec((None, mo, SUBLANES, ct), lambda b, s, j: (seq0 + b, 0, s, j))
    in_specs = [pl.BlockSpec(f3.shape, lambda b, s, j: (0, 0)), spec_in, spec_in, spec_tok, spec_tok,
                pl.BlockSpec((1, ct), lambda b, s, j: (0, j))]
    args = [f3, br, bi, u, x0, bias.reshape(1, c)]
    aliases = _alias_prev(in_specs, args, None if prev is None else prev.reshape(u.shape))
    return pl.pallas_call(
        _dft_outer_inv_kernel,
        grid=(nseq, n2 // SUBLANES, c // ct),
        in_specs=in_specs,
        out_specs=spec_tok,
        out_shape=jax.ShapeDtypeStruct(u.shape, F32),
        input_output_aliases=aliases,
        compiler_params=_params("parallel", "parallel", "parallel"),
        name="dft_outer_inverse",
    )(*args)


def _hyena_long_conv_group(u, x0, k_two, bias, row0, nseq, seq_len, prev):
    m, c = u.shape
    consts = _dft_constants(seq_len)
    n1, n2 = consts["n1"], consts["n2"]
    ak = _dft_outer_forward(consts["f1_full"], k_two.reshape(1, n1, n2, c), 0, 1)
    kfr, kfi = _fft_mid(ak, consts)
    tok_view = (m // seq_len, n1 // 2, n2, c)
    seq0 = row0 // seq_len
    a = _dft_outer_forward(consts["f1_half"], u.reshape(tok_view), seq0, nseq)
    br, bi = _fft_mid(a, consts, kf=(kfr, kfi))
    y = _dft_outer_inverse(consts["f3"], br, bi, u.reshape(tok_view), x0.reshape(tok_view), bias, seq0, prev)
    return y.reshape(m, c)


def _softplus(z):
    return jnp.maximum(z, 0.0) + jnp.log(1.0 + jnp.exp(-jnp.abs(z)))


def _lru_kernel(pl_ref, pg_ref, cw_ref, cb_ref, wa_ref, ba_ref, wx_ref, bx_ref, lam_ref, *rest, tchunk):
    o_ref, xp_ref, hs_ref = rest[-3:]
    seq_len, cw = pl_ref.shape
    nchunk = seq_len // tchunk
    halo = SUBLANES
    xp_ref[0:halo, :] = jnp.zeros((halo, cw), F32)
    xp_ref[halo + seq_len:2 * halo + seq_len, :] = jnp.zeros((halo, cw), F32)
    xp_ref[halo:halo + seq_len, :] = pl_ref[...]

    row = lax.broadcasted_iota(jnp.int32, (tchunk, cw), 0)
    nsteps = int(math.log2(tchunk))
    neg_c_softplus = -LRU_C * _softplus(-lam_ref[...])

    def gates(c, direction):
        base = pl.multiple_of(c * tchunk, tchunk)
        xw = xp_ref[pl.ds(base, tchunk + 2 * halo), :]
        ext = tchunk + 2 * halo
        xb = cb_ref[...]
        for j in range(LRU_CONV):
            shift = (LRU_CONV // 2 - j) % ext
            tap = xw if shift == 0 else pltpu.roll(xw, shift, 0)
            xb = xb + tap[halo:halo + tchunk, :] * cw_ref[j:j + 1, :]
        xb16 = xb.astype(BF16)
        r = jax.nn.sigmoid(jnp.dot(xb16, wa_ref[direction], preferred_element_type=F32)
                           + ba_ref[direction:direction + 1, :])
        i = jax.nn.sigmoid(jnp.dot(xb16, wx_ref[direction], preferred_element_type=F32)
                           + bx_ref[direction:direction + 1, :])
        log_a = r * neg_c_softplus[direction:direction + 1, :]
        a = jnp.exp(log_a)
        gain = jnp.sqrt(-jnp.tanh(log_a) * (a * a + 1.0))
        return a, gain * (i * xb)

    def scan(a, b, reverse):
        for s in range(nsteps):
            d = 1 << s
            if reverse:
                keep = row < tchunk - d
                a_sh = pltpu.roll(a, tchunk - d, 0)
                b_sh = pltpu.roll(b, tchunk - d, 0)
            else:
                keep = row >= d
                a_sh = pltpu.roll(a, d, 0)
                b_sh = pltpu.roll(b, d, 0)
            b = jnp.where(keep, a * b_sh + b, b)
            a = jnp.where(keep, a * a_sh, a)
        return a, b

    def fwd_body(c, h):
        a, b = gates(c, 0)
        a, b = scan(a, b, False)
        hc = a * h + b
        hs_ref[pl.ds(pl.multiple_of(c * tchunk, tchunk), tchunk), :] = hc
        return hc[tchunk - 1:tchunk, :]

    lax.fori_loop(0, nchunk, fwd_body, jnp.zeros((1, cw), F32))

    def rev_body(cc, h):
        c = nchunk - 1 - cc
        a, b = gates(c, 1)
        a, b = scan(a, b, True)
        hc = a * h + b
        sl = pl.ds(pl.multiple_of(c * tchunk, tchunk), tchunk)
        o_ref[sl, :] = (hs_ref[sl, :] + hc) * jax.nn.gelu(pg_ref[sl, :])
        return hc[0:1, :]

    lax.fori_loop(0, nchunk, rev_body, jnp.zeros((1, cw), F32))


def _rg_lru_group(p, col_gate, col_x, conv_w, conv_b, wa, ba, wx, bx, lam, row0, nseq, seq_len, prev, *,
                  tchunk=128):
    m = p.shape[0]
    nblk = wa.shape[1]
    cw = wa.shape[2]
    tchunk = min(tchunk, seq_len)
    sb0 = row0 // seq_len
    gb0 = col_gate // cw
    xb0 = col_x // cw
    vec = lambda rows: pl.BlockSpec((rows, cw), lambda b, n: (0, n))
    mat = pl.BlockSpec((2, None, cw, cw), lambda b, n: (0, n, 0, 0))
    in_specs = [
        pl.BlockSpec((seq_len, cw), lambda b, n: (sb0 + b, xb0 + n)),
        pl.BlockSpec((seq_len, cw), lambda b, n: (sb0 + b, gb0 + n)),
        vec(LRU_CONV), vec(1), mat, vec(2), mat, vec(2), vec(2),
    ]
    args = [p, p, conv_w, conv_b.reshape(1, -1), wa, ba, wx, bx, lam]
    aliases = _alias_prev(in_specs, args, prev)
    return pl.pallas_call(
        functools.partial(_lru_kernel, tchunk=tchunk),
        grid=(nseq, nblk),
        in_specs=in_specs,
        out_specs=pl.BlockSpec((seq_len, cw), lambda b, n: (sb0 + b, n)),
        out_shape=jax.ShapeDtypeStruct((m, nblk * cw), F32),
        input_output_aliases=aliases,
        scratch_shapes=[pltpu.VMEM((seq_len + 2 * SUBLANES, cw), F32), pltpu.VMEM((seq_len, cw), F32)],
        compiler_params=_params("parallel", "parallel"),
        name="rg_lru",
    )(*args)


def kernel(x_prompt, x_sample, ffn_norm, ffn_w1, ffn_w3, ffn_w2, mix_norm, final_norm, even_w_in, hyena_conv_w, hyena_conv_b, hyena_filt_w_in, hyena_filt_b, hyena_filt_w_hid, hyena_filt_freq, hyena_filt_w_out, hyena_bias, lru_conv_w, lru_conv_b, lru_wa, lru_ba, lru_wx, lru_bx, lru_lambda, even_w_out, attn_w_qkv, attn_lambda, attn_subln, attn_w_o):
    d = x_prompt.shape[-1]
    depth = ffn_norm.shape[0]
    groups = []
    row0 = 0
    for xg in (x_prompt, x_sample):
        groups.append((row0, xg.shape[0], xg.shape[1]))
        row0 += xg.shape[0] * xg.shape[1]
    groups = tuple(groups)
    x = jnp.concatenate([x_prompt.reshape(-1, d), x_sample.reshape(-1, d)], axis=0)

    hw = hyena_bias.shape[1]
    head_dim = d // (2 * ATT_HEADS)
    rope = _rope_tables(max(l for _, _, l in groups), head_dim)
    bf = lambda w: w.astype(BF16)

    for i in range(depth):
        j = i // 2
        x = _ffn(x, ffn_norm[i, 0], bf(ffn_w1[i, 0]), bf(ffn_w3[i, 0]), bf(ffn_w2[i, 0]))
        if i % 2 == 0:
            p = _norm_matmul(x, mix_norm[i], bf(even_w_in[j]))
            x0, u = _hyena_pre(p, hyena_conv_w[j], hyena_conv_b[j], groups, hw)
            wa, wx = bf(lru_wa[j]), bf(lru_wx[j])
            yh = yl = None
            for row0, nseq, seq_len in groups:
                k_two = _hyena_filter(seq_len, hyena_filt_w_in[j], hyena_filt_b[j], hyena_filt_w_hid[j],
                                      hyena_filt_freq[j], hyena_filt_w_out[j])
                yh = _hyena_long_conv_group(u, x0, k_two, hyena_bias[j], row0, nseq, seq_len, yh)
                yl = _rg_lru_group(p, 3 * hw, 3 * hw + wa.shape[1] * wa.shape[2], lru_conv_w[j], lru_conv_b[j],
                                   wa, lru_ba[j], wx, lru_bx[j], lru_lambda[j], row0, nseq, seq_len, yl)
            x = _proj_residual(x, [yh, yl], bf(even_w_out[j]))
        else:
            lambda_init = 0.8 - 0.6 * math.exp(-0.3 * i)
            qkv = _qkv(x, mix_norm[i], bf(attn_w_qkv[j]), rope, groups, head_dim=head_dim)
            o = None
            for row0, nseq, seq_len in groups:
                o = _diff_attention_group(qkv, attn_lambda[j], attn_subln[j], row0, nseq, seq_len, lambda_init, o)
            x = _proj_residual(x, [o], bf(attn_w_o[j]))
        x = _ffn(x, ffn_norm[i, 1], bf(ffn_w1[i, 1]), bf(ffn_w3[i, 1]), bf(ffn_w2[i, 1]))

    y = _final_norm(x, final_norm)
    outs = []
    for (row0, nseq, seq_len), xg in zip(groups, (x_prompt, x_sample)):
        outs.append(y[row0:row0 + nseq * seq_len].reshape(xg.shape))
    return tuple(outs)
```

```python
import functools
import math

import numpy as np
import jax
import jax.numpy as jnp
from jax import lax
from jax.experimental import pallas as pl
from jax.experimental.pallas import tpu as pltpu

F32 = jnp.float32
BF16 = jnp.bfloat16

NORM_EPS = 1e-6
HYENA_SHORT = 3
HYENA_EMB = 33
HYENA_BANDS = (HYENA_EMB - 1) // 2
HYENA_INNER = 2
HYENA_TARGET = 1e-2
HYENA_FAST = 0.3
HYENA_SLOW = 1.5
LRU_BLOCKS = 8
LRU_CONV = 4
LRU_C = 8.0
ATT_HEADS = 16
ROPE_THETA = 10000.0
LOG2E = 1.4426950408889634

LANES = 128
SUBLANES = 8
VMEM_LIMIT_BYTES = 56 * 1024 * 1024
FFT_INNER = 128


def _params(*sem):
    return pltpu.CompilerParams(dimension_semantics=sem, vmem_limit_bytes=VMEM_LIMIT_BYTES)


def _rms(x, g):
    ms = jnp.mean(x * x, axis=-1, keepdims=True)
    return x * lax.rsqrt(ms + NORM_EPS) * g


def _ffn_kernel(x_ref, g_ref, w1_ref, w3_ref, w2_ref, o_ref, xn_ref, acc_ref):
    f = pl.program_id(1)

    @pl.when(f == 0)
    def _():
        xn_ref[...] = _rms(x_ref[...], g_ref[...]).astype(BF16)
        acc_ref[...] = jnp.zeros_like(acc_ref)

    xn = xn_ref[...]
    a = jnp.dot(xn, w1_ref[...], preferred_element_type=F32)
    b = jnp.dot(xn, w3_ref[...], preferred_element_type=F32)
    h = (a * jax.nn.sigmoid(a) * b).astype(BF16)
    acc_ref[...] += jnp.dot(h, w2_ref[...], preferred_element_type=F32)

    @pl.when(f == pl.num_programs(1) - 1)
    def _():
        o_ref[...] = x_ref[...] + 0.5 * acc_ref[...]


def _ffn(x, g, w1, w3, w2, *, tm=512, tf=512):
    m, d = x.shape
    ff = w1.shape[1]
    tm = min(tm, m)
    tf = min(tf, ff)
    return pl.pallas_call(
        _ffn_kernel,
        grid=(m // tm, ff // tf),
        in_specs=[
            pl.BlockSpec((tm, d), lambda i, f: (i, 0)),
            pl.BlockSpec((1, d), lambda i, f: (0, 0)),
            pl.BlockSpec((d, tf), lambda i, f: (0, f)),
            pl.BlockSpec((d, tf), lambda i, f: (0, f)),
            pl.BlockSpec((tf, d), lambda i, f: (f, 0)),
        ],
        out_specs=pl.BlockSpec((tm, d), lambda i, f: (i, 0)),
        out_shape=jax.ShapeDtypeStruct((m, d), F32),
        scratch_shapes=[pltpu.VMEM((tm, d), BF16), pltpu.VMEM((tm, d), F32)],
        compiler_params=_params("parallel", "arbitrary"),
        name="ffn",
    )(x, g.reshape(1, d), w1, w3, w2)


def _norm_matmul_kernel(x_ref, g_ref, w_ref, o_ref, xn_ref):
    @pl.when(pl.program_id(1) == 0)
    def _():
        xn_ref[...] = _rms(x_ref[...], g_ref[...]).astype(BF16)

    o_ref[...] = jnp.dot(xn_ref[...], w_ref[...], preferred_element_type=F32).astype(o_ref.dtype)


def _norm_matmul(x, g, w, *, out_dtype=F32, tm=1024, tn=512):
    m, d = x.shape
    n = w.shape[1]
    tm = min(tm, m)
    tn = min(tn, n)
    return pl.pallas_call(
        _norm_matmul_kernel,
        grid=(m // tm, n // tn),
        in_specs=[
            pl.BlockSpec((tm, d), lambda i, j: (i, 0)),
            pl.BlockSpec((1, d), lambda i, j: (0, 0)),
            pl.BlockSpec((d, tn), lambda i, j: (0, j)),
        ],
        out_specs=pl.BlockSpec((tm, tn), lambda i, j: (i, j)),
        out_shape=jax.ShapeDtypeStruct((m, n), out_dtype),
        scratch_shapes=[pltpu.VMEM((tm, d), BF16)],
        compiler_params=_params("parallel", "arbitrary"),
        name="norm_matmul",
    )(x, g.reshape(1, d), w)


def _rope_tables(lmax, head_dim):
    half = head_dim // 2
    inv = ROPE_THETA ** (-np.arange(0, head_dim, 2, dtype=np.float64) / head_dim)
    ang = np.arange(lmax, dtype=np.float64)[:, None] * inv[None, :]
    ang = np.concatenate([ang, ang], axis=-1)
    cos = np.tile(np.cos(ang), (1, LANES // head_dim))
    sin = np.tile(np.sin(ang), (1, LANES // head_dim))
    first = (np.arange(LANES) % head_dim) < half
    sin_up = np.where(first[None, :], -sin, 0.0)
    sin_dn = np.where(first[None, :], 0.0, sin)
    q_scale = head_dim ** -0.5 * LOG2E
    one, zero = np.ones_like(cos), np.zeros_like(cos)
    stack = lambda q, k, v: jnp.asarray(np.stack([q, k, v]), F32)
    return (stack(cos * q_scale, cos, one), stack(sin_up * q_scale, sin_up, zero),
            stack(sin_dn * q_scale, sin_dn, zero))


def _qkv_kernel(x_ref, g_ref, w_ref, cos_ref, sup_ref, sdn_ref, o_ref, xn_ref, *, half):
    @pl.when(pl.program_id(1) == 0)
    def _():
        xn_ref[...] = _rms(x_ref[...], g_ref[...]).astype(BF16)

    res = jnp.dot(xn_ref[...], w_ref[...], preferred_element_type=F32)
    cos, sup, sdn = cos_ref[...], sup_ref[...], sdn_ref[...]
    for c in range(res.shape[1] // LANES):
        blk = res[:, c * LANES:(c + 1) * LANES]
        rot = blk * cos + pltpu.roll(blk, LANES - half, 1) * sup + pltpu.roll(blk, half, 1) * sdn
        o_ref[:, c * LANES:(c + 1) * LANES] = rot.astype(o_ref.dtype)


def _qkv(x, g, w, tables, groups, *, head_dim, tm=1024, tn=512):
    m, d = x.shape
    n = w.shape[1]
    tm = min(tm, min(l for _, _, l in groups))
    n_sec = d // tn
    cos, sup, sdn = tables

    def table_block(i, j):
        blk = 0
        for row0, nseq, l in groups:
            inside = jnp.logical_and(i * tm >= row0, i * tm < row0 + nseq * l)
            blk = jnp.where(inside, ((i * tm - row0) % l) // tm, blk)
        return (j // n_sec, blk, 0)

    table_spec = pl.BlockSpec((None, tm, LANES), table_block)
    return pl.pallas_call(
        functools.partial(_qkv_kernel, half=head_dim // 2),
        grid=(m // tm, n // tn),
        in_specs=[
            pl.BlockSpec((tm, d), lambda i, j: (i, 0)),
            pl.BlockSpec((1, d), lambda i, j: (0, 0)),
            pl.BlockSpec((d, tn), lambda i, j: (0, j)),
            table_spec, table_spec, table_spec,
        ],
        out_specs=pl.BlockSpec((tm, tn), lambda i, j: (i, j)),
        out_shape=jax.ShapeDtypeStruct((m, n), BF16),
        scratch_shapes=[pltpu.VMEM((tm, d), BF16)],
        compiler_params=_params("parallel", "arbitrary"),
        name="qkv_rope",
    )(x, g.reshape(1, d), w, cos, sup, sdn)


def _attn_kernel(lam_ref, g_ref, q_ref, k_ref, v_ref, *rest, tk, tq, lambda_init):
    o_ref, vt_ref, acc_ref, sa_ref, sb_ref = rest[-5:]
    seq_len = k_ref.shape[0]
    nchunk = seq_len // tk
    nsub = q_ref.shape[0] // tq
    half = q_ref.shape[1] // 2

    @pl.when(pl.program_id(2) == 0)
    def _():
        for c in range(nchunk):
            vt_ref[c] = v_ref[c * tk:(c + 1) * tk, :].astype(F32).T.astype(BF16)

    lane = lax.broadcasted_iota(jnp.int32, (tq, q_ref.shape[1]), 1)
    lv = lam_ref[...]
    lam = (jnp.exp(jnp.sum(lv[0:1] * lv[1:2], axis=1, keepdims=True))
           - jnp.exp(jnp.sum(lv[2:3] * lv[3:4], axis=1, keepdims=True)) + lambda_init)

    def rows(j):
        return pl.ds(j * tq if isinstance(j, int) else pl.multiple_of(j * tq, tq), tq)

    def q_parts(j):
        q = q_ref[rows(j), :]
        zero = jnp.zeros_like(q)
        return jnp.where(lane < half, q, zero), jnp.where(lane >= half, q, zero)

    def score_step(c, ms, qs, s_out):
        k = k_ref[pl.ds(pl.multiple_of(c * tk, tk), tk), :]
        new = []
        for comp in range(2):
            s = lax.dot_general(k, qs[comp], (((1,), (1,)), ((), ())), preferred_element_type=F32)
            s_out[comp, c] = s
            new.append(jnp.maximum(ms[comp], jnp.max(s, axis=0, keepdims=True)))
        return tuple(new)

    def pv_step(c, ls, ms, s_in):
        vt = vt_ref[c]
        new = []
        for comp in range(2):
            p = jnp.exp2(s_in[comp, c] - ms[comp])
            new.append(ls[comp] + jnp.sum(p, axis=0, keepdims=True))
            acc_ref[comp] += jnp.dot(vt, p.astype(BF16), preferred_element_type=F32)
        return tuple(new)

    def finalize(j, ls):
        ot = acc_ref[0] * (1.0 / ls[0]) - acc_ref[1] * (lam * (1.0 / ls[1]))
        ms2 = jnp.mean(ot * ot, axis=0, keepdims=True)
        ot = ot * lax.rsqrt(ms2 + NORM_EPS) * g_ref[...] * (1.0 - lambda_init)
        o_ref[rows(j), :] = ot.T.astype(o_ref.dtype)

    neg = (jnp.full((1, tq), -jnp.inf, F32),) * 2
    zero_rows = (jnp.zeros((1, tq), F32),) * 2

    qs0 = q_parts(0)
    ms0 = lax.fori_loop(0, nchunk, lambda c, m: score_step(c, m, qs0, sa_ref), neg, unroll=min(4, nchunk))

    def fused(j, ms, s_in, s_out):
        acc_ref[...] = jnp.zeros_like(acc_ref)
        qn = q_parts(j + 1)

        def both(c, carry):
            ls, mn = carry
            return pv_step(c, ls, ms, s_in), score_step(c, mn, qn, s_out)

        ls, ms_next = lax.fori_loop(0, nchunk, both, (zero_rows, neg), unroll=min(8, nchunk))
        finalize(j, ls)
        return ms_next

    def pair(jj, ms):
        return fused(2 * jj + 1, fused(2 * jj, ms, sa_ref, sb_ref), sb_ref, sa_ref)

    ms_last = lax.fori_loop(0, (nsub - 1) // 2, pair, ms0)
    if (nsub - 1) % 2:
        ms_last = fused(nsub - 2, ms_last, sa_ref, sb_ref)
    last = nsub - 1
    s_last = sb_ref if last % 2 else sa_ref
    acc_ref[...] = jnp.zeros_like(acc_ref)
    ls = lax.fori_loop(0, nchunk, lambda c, l: pv_step(c, l, ms_last, s_last), zero_rows, unroll=min(4, nchunk))
    finalize(last, ls)


def _alias_prev(in_specs, args, prev):
    if prev is None:
        return {}
    in_specs.append(pl.BlockSpec(memory_space=pl.ANY))
    args.append(prev)
    return {len(args) - 1: 0}


def _diff_attention_group(qkv, lam_vec, subln_g, row0, nseq, seq_len, lambda_init, prev, *, tq=256, nsub=16,
                          tk=512):
    m = qkv.shape[0]
    d = qkv.shape[1] // 3
    hw = d // ATT_HEADS
    tq = min(tq, seq_len)
    tk = min(tk, seq_len)
    bq = min(nsub * tq, seq_len)
    nq = seq_len // bq
    qb0 = row0 // bq
    kb0 = row0 // seq_len
    in_specs = [
        pl.BlockSpec(lam_vec.shape, lambda b, h, i: (0, 0)),
        pl.BlockSpec((hw, 1), lambda b, h, i: (0, 0)),
        pl.BlockSpec((bq, hw), lambda b, h, i: (qb0 + b * nq + i, h)),
        pl.BlockSpec((seq_len, hw), lambda b, h, i: (kb0 + b, ATT_HEADS + h)),
        pl.BlockSpec((seq_len, hw), lambda b, h, i: (kb0 + b, 2 * ATT_HEADS + h)),
    ]
    args = [lam_vec, subln_g.reshape(hw, 1), qkv, qkv, qkv]
    aliases = _alias_prev(in_specs, args, prev)
    return pl.pallas_call(
        functools.partial(_attn_kernel, tk=tk, tq=tq, lambda_init=lambda_init),
        grid=(nseq, ATT_HEADS, nq),
        in_specs=in_specs,
        out_specs=pl.BlockSpec((bq, hw), lambda b, h, i: (qb0 + b * nq + i, h)),
        out_shape=jax.ShapeDtypeStruct((m, d), BF16),
        input_output_aliases=aliases,
        scratch_shapes=[
            pltpu.VMEM((seq_len // tk, hw, tk), BF16),
            pltpu.VMEM((2, hw, tq), F32),
            pltpu.VMEM((2, seq_len // tk, tk, tq), F32),
            pltpu.VMEM((2, seq_len // tk, tk, tq), F32),
        ],
        compiler_params=_params("parallel", "parallel", "arbitrary"),
        name="diff_attention",
    )(*args)


def _proj_res_kernel(*refs, nparts):
    res_ref = refs[0]
    a_refs = refs[1:1 + nparts]
    w_refs = refs[1 + nparts:1 + 2 * nparts]
    o_ref = refs[1 + 2 * nparts]
    acc = res_ref[...]
    for a_ref, w_ref in zip(a_refs, w_refs):
        acc = acc + jnp.dot(a_ref[...].astype(BF16), w_ref[...], preferred_element_type=F32)
    o_ref[...] = acc


def _proj_residual(res, parts, w, *, tm=512, tn=2048):
    m, n = res.shape
    tm = min(tm, m)
    tn = min(tn, n)
    nparts = len(parts)
    kp = w.shape[0] // nparts
    in_specs = [pl.BlockSpec((tm, tn), lambda i, j: (i, j))]
    in_specs += [pl.BlockSpec((tm, kp), lambda i, j: (i, 0)) for _ in parts]
    in_specs += [pl.BlockSpec((kp, tn), functools.partial(lambda i, j, p: (p, j), p=p)) for p in range(nparts)]
    return pl.pallas_call(
        functools.partial(_proj_res_kernel, nparts=nparts),
        grid=(m // tm, n // tn),
        in_specs=in_specs,
        out_specs=pl.BlockSpec((tm, tn), lambda i, j: (i, j)),
        out_shape=jax.ShapeDtypeStruct((m, n), F32),
        compiler_params=_params("parallel", "arbitrary"),
        name="proj_residual",
    )(res, *parts, *([w] * nparts))


def _final_norm_kernel(x_ref, g_ref, o_ref):
    o_ref[...] = _rms(x_ref[...], g_ref[...])


def _final_norm(x, g, *, tm=512):
    m, d = x.shape
    tm = min(tm, m)
    return pl.pallas_call(
        _final_norm_kernel,
        grid=(m // tm,),
        in_specs=[pl.BlockSpec((tm, d), lambda i: (i, 0)), pl.BlockSpec((1, d), lambda i: (0, 0))],
        out_specs=pl.BlockSpec((tm, d), lambda i: (i, 0)),
        out_shape=jax.ShapeDtypeStruct((m, d), F32),
        compiler_params=_params("parallel"),
        name="final_norm",
    )(x, g.reshape(1, d))


def _is_any(i, values):
    hit = i == values[0]
    for v in values[1:]:
        hit = jnp.logical_or(hit, i == v)
    return hit


def _hyena_pre_kernel(*refs, tt, first_tiles, last_tiles):
    (x0m, x0p, x0n, x1m, x1p, x1n, vm, vp, vn, w0, w1, w2, b0, b1, b2, x0_out, u_out) = refs
    i = pl.program_id(0)
    first = _is_any(i, first_tiles)
    last = _is_any(i, last_tiles)

    def conv(m_ref, p_ref, n_ref, w_ref, b_ref):
        x = m_ref[...]
        row = lax.broadcasted_iota(jnp.int32, x.shape, 0)
        prev_row = jnp.where(first, 0.0, p_ref[SUBLANES - 1:SUBLANES, :])
        next_row = jnp.where(last, 0.0, n_ref[0:1, :])
        xm = jnp.where(row == 0, prev_row, pltpu.roll(x, 1, 0))
        xp = jnp.where(row == tt - 1, next_row, pltpu.roll(x, tt - 1, 0))
        return b_ref[...] + xm * w_ref[0:1, :] + x * w_ref[1:2, :] + xp * w_ref[2:3, :]

    x0_out[...] = conv(x0m, x0p, x0n, w0, b0)
    u_out[...] = conv(vm, vp, vn, w2, b2) * conv(x1m, x1p, x1n, w1, b1)


def _hyena_pre(p, conv_w, conv_b, groups, width, *, tt=512, ct=512):
    m = p.shape[0]
    tt = min(tt, min(l for _, _, l in groups))
    ct = min(ct, width)
    nc = width // ct
    first_tiles = tuple((row0 + s * l) // tt for row0, nseq, l in groups for s in range(nseq))
    last_tiles = tuple((row0 + (s + 1) * l) // tt - 1 for row0, nseq, l in groups for s in range(nseq))
    hb = tt // SUBLANES
    nhb = m // SUBLANES

    def main(sec):
        return pl.BlockSpec((tt, ct), lambda i, j: (i, sec * nc + j))

    def prev(sec):
        return pl.BlockSpec((SUBLANES, ct), lambda i, j: (jnp.maximum(i * hb - 1, 0), sec * nc + j))

    def nxt(sec):
        return pl.BlockSpec((SUBLANES, ct), lambda i, j: (jnp.minimum((i + 1) * hb, nhb - 1), sec * nc + j))

    def wspec(sec):
        return pl.BlockSpec((HYENA_SHORT, ct), lambda i, j: (0, sec * nc + j))

    def bspec(sec):
        return pl.BlockSpec((1, ct), lambda i, j: (0, sec * nc + j))

    in_specs = []
    for sec in range(3):
        in_specs += [main(sec), prev(sec), nxt(sec)]
    in_specs += [wspec(0), wspec(1), wspec(2), bspec(0), bspec(1), bspec(2)]
    out_spec = pl.BlockSpec((tt, ct), lambda i, j: (i, j))
    cb = conv_b.reshape(1, -1)
    return pl.pallas_call(
        functools.partial(_hyena_pre_kernel, tt=tt, first_tiles=first_tiles, last_tiles=last_tiles),
        grid=(m // tt, nc),
        in_specs=in_specs,
        out_specs=[out_spec, out_spec],
        out_shape=[jax.ShapeDtypeStruct((m, width), F32)] * 2,
        compiler_params=_params("parallel", "arbitrary"),
        name="hyena_short_conv",
    )(*([p] * 9), conv_w, conv_w, conv_w, cb, cb, cb)


def _hyena_positions(seq_len):
    pos = np.arange(seq_len, dtype=np.float64)[:, None]
    t = pos / max(seq_len - 1, 1)
    w = (2.0 * math.pi / seq_len) * pos
    bands = np.linspace(1e-4, HYENA_BANDS - 1, HYENA_BANDS, dtype=np.float64)[None, :]
    z = np.concatenate([t, np.cos(bands * w), -np.sin(bands * w)], axis=-1)
    z2 = np.concatenate([z, z[:1], z[:0:-1]], axis=0)
    out = np.zeros((2 * seq_len, 64), np.float32)
    out[:, :HYENA_EMB] = z2
    return jnp.asarray(out)


def _filter_kernel(z_ref, win_ref, b_ref, whid_ref, fr_ref, wout_ref, dl_ref, o_ref, *, seq_len):
    tr = z_ref.shape[0]
    hp = lax.Precision.HIGHEST
    z = z_ref[...]
    fr = fr_ref[...]
    h = jnp.sin(fr * (jnp.dot(z, win_ref[...], precision=hp, preferred_element_type=F32) + b_ref[0:1, :]))
    for j in range(HYENA_INNER):
        h = jnp.sin(fr * (jnp.dot(h, whid_ref[j], precision=hp, preferred_element_type=F32)
                          + b_ref[j + 1:j + 2, :]))
    k = jnp.dot(h, wout_ref[...], precision=hp, preferred_element_type=F32)
    k = k * jnp.exp(-z[:, 0:1] * dl_ref[...])
    row = pl.program_id(0) * tr + lax.broadcasted_iota(jnp.int32, k.shape, 0)
    o_ref[...] = jnp.where(row == seq_len, 0.0, k)


def _hyena_filter(seq_len, w_in, b, w_hid, freq, w_out, *, tr=512):
    order = w_in.shape[1]
    width = w_out.shape[1] // 2
    tr = min(tr, seq_len)
    nfwd = seq_len // tr
    z2 = _hyena_positions(seq_len)
    w_in_p = jnp.zeros((z2.shape[1], order), F32).at[:HYENA_EMB].set(w_in)
    deltas = np.abs(np.linspace(math.log(HYENA_TARGET) / HYENA_SLOW, math.log(HYENA_TARGET) / HYENA_FAST,
                                width, dtype=np.float64))[None, :].astype(np.float32)
    return pl.pallas_call(
        functools.partial(_filter_kernel, seq_len=seq_len),
        grid=(2 * nfwd,),
        in_specs=[
            pl.BlockSpec((tr, z2.shape[1]), lambda i: (i, 0)),
            pl.BlockSpec(w_in_p.shape, lambda i: (0, 0)),
            pl.BlockSpec(b.shape, lambda i: (0, 0)),
            pl.BlockSpec(w_hid.shape, lambda i: (0, 0, 0)),
            pl.BlockSpec((1, order), lambda i: (0, 0)),
            pl.BlockSpec((order, width), lambda i: (0, i // nfwd)),
            pl.BlockSpec((1, width), lambda i: (0, 0)),
        ],
        out_specs=pl.BlockSpec((tr, width), lambda i: (i, 0)),
        out_shape=jax.ShapeDtypeStruct((2 * seq_len, width), F32),
        compiler_params=_params("parallel"),
        name="hyena_filter",
    )(z2, w_in_p, b, w_hid, freq.reshape(1, order), w_out, jnp.asarray(deltas))


def _dft_constants(seq_len):
    n = 2 * seq_len
    n2 = FFT_INNER
    n1 = n // n2
    a1 = 2.0 * np.pi * np.outer(np.arange(n1), np.arange(n1)) / n1
    f1 = np.concatenate([np.cos(a1), -np.sin(a1)], axis=0)
    f3 = np.concatenate([np.cos(a1), -np.sin(a1)], axis=1)[:n1 // 2]
    a2 = 2.0 * np.pi * np.outer(np.arange(n2), np.arange(n2)) / n2
    gr, gi = np.cos(a2), -np.sin(a2)
    g_fwd = np.block([[gr, -gi], [gi, gr]])
    g_inv = np.block([[gr, gi], [-gi, gr]])
    at = 2.0 * np.pi * np.outer(np.arange(n1), np.arange(n2)) / n
    tw_r = np.cos(at)[:, :, None]
    tw_i = -np.sin(at)[:, :, None]
    c = lambda a, dt: jnp.asarray(a, dt)
    return dict(n1=n1, n2=n2, f1_full=c(f1, BF16), f1_half=c(f1[:, :n1 // 2], BF16), f3=c(f3, BF16),
                g_fwd=c(g_fwd, BF16), g_inv=c(g_inv, BF16), tw_r=c(tw_r, F32), tw_i=c(tw_i, F32))


def _dft_outer_fwd_kernel(f_ref, x_ref, o_ref):
    f = f_ref[...]
    for s in range(x_ref.shape[1]):
        o_ref[:, s, :] = jnp.dot(f, x_ref[:, s, :].astype(BF16), preferred_element_type=F32)


def _dft_outer_forward(fmat, x, seq0, nseq, *, ct=1024):
    _, k, n2, c = x.shape
    mo = fmat.shape[0]
    ct = min(ct, c)
    return pl.pallas_call(
        _dft_outer_fwd_kernel,
        grid=(nseq, n2 // SUBLANES, c // ct),
        in_specs=[
            pl.BlockSpec((mo, k), lambda b, s, j: (0, 0)),
            pl.BlockSpec((None, k, SUBLANES, ct), lambda b, s, j: (seq0 + b, 0, s, j)),
        ],
        out_specs=pl.BlockSpec((None, mo, SUBLANES, ct), lambda b, s, j: (b, 0, s, j)),
        out_shape=jax.ShapeDtypeStruct((nseq, mo, n2, c), F32),
        compiler_params=_params("parallel", "parallel", "parallel"),
        name="dft_outer_forward",
    )(fmat, x)


def _twiddle_dft(ar_ref, ai_ref, twr_ref, twi_ref, g_ref):
    n2 = ar_ref.shape[0]
    ar, ai = ar_ref[...], ai_ref[...]
    tr, ti = twr_ref[...], twi_ref[...]
    z = jnp.concatenate([ar * tr - ai * ti, ar * ti + ai * tr], axis=0).astype(BF16)
    x = jnp.dot(g_ref[...], z, preferred_element_type=F32)
    return x[:n2], x[n2:]


def _spectrum_kernel(ar_ref, ai_ref, twr_ref, twi_ref, g_ref, kr_ref, ki_ref, *, scale):
    xr, xi = _twiddle_dft(ar_ref, ai_ref, twr_ref, twi_ref, g_ref)
    kr_ref[...] = xr * scale
    ki_ref[...] = xi * scale


def _fft_mid_kernel(ar_ref, ai_ref, twr_ref, twi_ref, g_ref, gi_ref, kr_ref, ki_ref, br_ref, bi_ref):
    n2 = ar_ref.shape[0]
    xr, xi = _twiddle_dft(ar_ref, ai_ref, twr_ref, twi_ref, g_ref)
    kr, ki = kr_ref[...], ki_ref[...]
    y = jnp.concatenate([xr * kr - xi * ki, xr * ki + xi * kr], axis=0).astype(BF16)
    b = jnp.dot(gi_ref[...], y, preferred_element_type=F32)
    br, bi = b[:n2], b[n2:]
    tr, ti = twr_ref[...], twi_ref[...]
    br_ref[...] = br * tr + bi * ti
    bi_ref[...] = bi * tr - br * ti


def _fft_mid(a, consts, kf=None, *, ct=1024):
    nb, _, n2, c = a.shape
    n1 = consts["n1"]
    ct = min(ct, c)
    grid = (nb, n1, c // ct)
    blk = lambda im: pl.BlockSpec((None, None, n2, ct), lambda b, k, j: (b, im * n1 + k, 0, j))
    tw = pl.BlockSpec((None, n2, 1), lambda b, k, j: (k, 0, 0))
    gs = pl.BlockSpec((2 * n2, 2 * n2), lambda b, k, j: (0, 0))
    half_spec = pl.BlockSpec((None, None, n2, ct), lambda b, k, j: (b, k, 0, j))
    half_shape = jax.ShapeDtypeStruct((nb, n1, n2, c), F32)
    if kf is None:
        return pl.pallas_call(
            functools.partial(_spectrum_kernel, scale=1.0 / (n1 * n2)),
            grid=grid,
            in_specs=[blk(0), blk(1), tw, tw, gs],
            out_specs=[half_spec, half_spec],
            out_shape=[half_shape, half_shape],
            compiler_params=_params("parallel", "parallel", "parallel"),
            name="dft_filter_spectrum",
        )(a, a, consts["tw_r"], consts["tw_i"], consts["g_fwd"])
    kblk = lambda: pl.BlockSpec((None, None, n2, ct), lambda b, k, j: (0, k, 0, j))
    return pl.pallas_call(
        _fft_mid_kernel,
        grid=grid,
        in_specs=[blk(0), blk(1), tw, tw, gs, gs, kblk(), kblk()],
        out_specs=[half_spec, half_spec],
        out_shape=[half_shape, half_shape],
        compiler_params=_params("parallel", "parallel", "parallel"),
        name="dft_inner_filter",
    )(a, a, consts["tw_r"], consts["tw_i"], consts["g_fwd"], consts["g_inv"], kf[0], kf[1])


def _dft_outer_inv_kernel(f_ref, br_ref, bi_ref, u_ref, x0_ref, bias_ref, *rest):
    o_ref = rest[-1]
    f = f_ref[...]
    bias = bias_ref[...]
    for s in range(u_ref.shape[1]):
        b = jnp.concatenate([br_ref[:, s, :], bi_ref[:, s, :]], axis=0).astype(BF16)
        y = jnp.dot(f, b, preferred_element_type=F32)
        o_ref[:, s, :] = x0_ref[:, s, :] * (y + u_ref[:, s, :] * bias)


def _dft_outer_inverse(f3, br, bi, u, x0, bias, seq0, prev, *, ct=1024):
    nseq, n1, n2, c = br.shape
    mo = f3.shape[0]
    ct = min(ct, c)
    spec_in = pl.BlockSpec((None, n1, SUBLANES, ct), lambda b, s, j: (b, 0, s, j))
    spec_tok = pl.BlockSpec((None, mo, SUBLANES, ct), lambda b, s, j: (seq0 + b, 0, s, j))
    in_specs = [pl.BlockSpec(f3.shape, lambda b, s, j: (0, 0)), spec_in, spec_in, spec_tok, spec_tok,
                pl.BlockSpec((1, ct), lambda b, s, j: (0, j))]
    args = [f3, br, bi, u, x0, bias.reshape(1, c)]
    aliases = _alias_prev(in_specs, args, None if prev is None else prev.reshape(u.shape))
    return pl.pallas_call(
        _dft_outer_inv_kernel,
        grid=(nseq, n2 // SUBLANES, c // ct),
        in_specs=in_specs,
        out_specs=spec_tok,
        out_shape=jax.ShapeDtypeStruct(u.shape, F32),
        input_output_aliases=aliases,
        compiler_params=_params("parallel", "parallel", "parallel"),
        name="dft_outer_inverse",
    )(*args)


def _hyena_long_conv_group(u, x0, k_two, bias, row0, nseq, seq_len, prev):
    m, c = u.shape
    consts = _dft_constants(seq_len)
    n1, n2 = consts["n1"], consts["n2"]
    ak = _dft_outer_forward(consts["f1_full"], k_two.reshape(1, n1, n2, c), 0, 1)
    kfr, kfi = _fft_mid(ak, consts)
    tok_view = (m // seq_len, n1 // 2, n2, c)
    seq0 = row0 // seq_len
    a = _dft_outer_forward(consts["f1_half"], u.reshape(tok_view), seq0, nseq)
    br, bi = _fft_mid(a, consts, kf=(kfr, kfi))
    y = _dft_outer_inverse(consts["f3"], br, bi, u.reshape(tok_view), x0.reshape(tok_view), bias, seq0, prev)
    return y.reshape(m, c)


def _softplus(z):
    return jnp.maximum(z, 0.0) + jnp.log(1.0 + jnp.exp(-jnp.abs(z)))


def _lru_kernel(pl_ref, pg_ref, cw_ref, cb_ref, wa_ref, ba_ref, wx_ref, bx_ref, lam_ref, *rest, tchunk):
    o_ref, xp_ref, hs_ref = rest[-3:]
    seq_len, cw = pl_ref.shape
    nchunk = seq_len // tchunk
    halo = SUBLANES
    xp_ref[0:halo, :] = jnp.zeros((halo, cw), F32)
    xp_ref[halo + seq_len:2 * halo + seq_len, :] = jnp.zeros((halo, cw), F32)
    xp_ref[halo:halo + seq_len, :] = pl_ref[...]

    row = lax.broadcasted_iota(jnp.int32, (tchunk, cw), 0)
    nsteps = int(math.log2(tchunk))
    neg_c_softplus = -LRU_C * _softplus(-lam_ref[...])

    def gates(c, direction):
        base = pl.multiple_of(c * tchunk, tchunk)
        xw = xp_ref[pl.ds(base, tchunk + 2 * halo), :]
        ext = tchunk + 2 * halo
        xb = cb_ref[...]
        for j in range(LRU_CONV):
            shift = (LRU_CONV // 2 - j) % ext
            tap = xw if shift == 0 else pltpu.roll(xw, shift, 0)
            xb = xb + tap[halo:halo + tchunk, :] * cw_ref[j:j + 1, :]
        xb16 = xb.astype(BF16)
        r = jax.nn.sigmoid(jnp.dot(xb16, wa_ref[direction], preferred_element_type=F32)
                           + ba_ref[direction:direction + 1, :])
        i = jax.nn.sigmoid(jnp.dot(xb16, wx_ref[direction], preferred_element_type=F32)
                           + bx_ref[direction:direction + 1, :])
        log_a = r * neg_c_softplus[direction:direction + 1, :]
        a = jnp.exp(log_a)
        gain = jnp.sqrt(-jnp.tanh(log_a) * (a * a + 1.0))
        return a, gain * (i * xb)

    def scan(a, b, reverse):
        for s in range(nsteps):
            d = 1 << s
            if reverse:
                keep = row < tchunk - d
                a_sh = pltpu.roll(a, tchunk - d, 0)
                b_sh = pltpu.roll(b, tchunk - d, 0)
            else:
                keep = row >= d
                a_sh = pltpu.roll(a, d, 0)
                b_sh = pltpu.roll(b, d, 0)
            b = jnp.where(keep, a * b_sh + b, b)
            a = jnp.where(keep, a * a_sh, a)
        return a, b

    def fwd_body(c, h):
        a, b = gates(c, 0)
        a, b = scan(a, b, False)
        hc = a * h + b
        hs_ref[pl.ds(pl.multiple_of(c * tchunk, tchunk), tchunk), :] = hc
        return hc[tchunk - 1:tchunk, :]

    lax.fori_loop(0, nchunk, fwd_body, jnp.zeros((1, cw), F32), unroll=min(4, nchunk))

    def rev_body(cc, h):
        c = nchunk - 1 - cc
        a, b = gates(c, 1)
        a, b = scan(a, b, True)
        hc = a * h + b
        sl = pl.ds(pl.multiple_of(c * tchunk, tchunk), tchunk)
        o_ref[sl, :] = (hs_ref[sl, :] + hc) * jax.nn.gelu(pg_ref[sl, :])
        return hc[0:1, :]

    lax.fori_loop(0, nchunk, rev_body, jnp.zeros((1, cw), F32), unroll=min(4, nchunk))


def _rg_lru_group(p, col_gate, col_x, conv_w, conv_b, wa, ba, wx, bx, lam, row0, nseq, seq_len, prev, *,
                  tchunk=128):
    m = p.shape[0]
    nblk = wa.shape[1]
    cw = wa.shape[2]
    tchunk = min(tchunk, seq_len)
    sb0 = row0 // seq_len
    gb0 = col_gate // cw
    xb0 = col_x // cw
    vec = lambda rows: pl.BlockSpec((rows, cw), lambda b, n: (0, n))
    mat = pl.BlockSpec((2, None, cw, cw), lambda b, n: (0, n, 0, 0))
    in_specs = [
        pl.BlockSpec((seq_len, cw), lambda b, n: (sb0 + b, xb0 + n)),
        pl.BlockSpec((seq_len, cw), lambda b, n: (sb0 + b, gb0 + n)),
        vec(LRU_CONV), vec(1), mat, vec(2), mat, vec(2), vec(2),
    ]
    args = [p, p, conv_w, conv_b.reshape(1, -1), wa, ba, wx, bx, lam]
    aliases = _alias_prev(in_specs, args, prev)
    return pl.pallas_call(
        functools.partial(_lru_kernel, tchunk=tchunk),
        grid=(nseq, nblk),
        in_specs=in_specs,
        out_specs=pl.BlockSpec((seq_len, cw), lambda b, n: (sb0 + b, n)),
        out_shape=jax.ShapeDtypeStruct((m, nblk * cw), F32),
        input_output_aliases=aliases,
        scratch_shapes=[pltpu.VMEM((seq_len + 2 * SUBLANES, cw), F32), pltpu.VMEM((seq_len, cw), F32)],
        compiler_params=_params("parallel", "parallel"),
        name="rg_lru",
    )(*args)


def kernel(x_prompt, x_sample, ffn_norm, ffn_w1, ffn_w3, ffn_w2, mix_norm, final_norm, even_w_in, hyena_conv_w, hyena_conv_b, hyena_filt_w_in, hyena_filt_b, hyena_filt_w_hid, hyena_filt_freq, hyena_filt_w_out, hyena_bias, lru_conv_w, lru_conv_b, lru_wa, lru_ba, lru_wx, lru_bx, lru_lambda, even_w_out, attn_w_qkv, attn_lambda, attn_subln, attn_w_o):
    d = x_prompt.shape[-1]
    depth = ffn_norm.shape[0]
    groups = []
    row0 = 0
    for xg in (x_prompt, x_sample):
        groups.append((row0, xg.shape[0], xg.shape[1]))
        row0 += xg.shape[0] * xg.shape[1]
    groups = tuple(groups)
    x = jnp.concatenate([x_prompt.reshape(-1, d), x_sample.reshape(-1, d)], axis=0)

    hw = hyena_bias.shape[1]
    head_dim = d // (2 * ATT_HEADS)
    rope = _rope_tables(max(l for _, _, l in groups), head_dim)
    bf = lambda w: w.astype(BF16)

    for i in range(depth):
        j = i // 2
        x = _ffn(x, ffn_norm[i, 0], bf(ffn_w1[i, 0]), bf(ffn_w3[i, 0]), bf(ffn_w2[i, 0]))
        if i % 2 == 0:
            p = _norm_matmul(x, mix_norm[i], bf(even_w_in[j]))
            x0, u = _hyena_pre(p, hyena_conv_w[j], hyena_conv_b[j], groups, hw)
            wa, wx = bf(lru_wa[j]), bf(lru_wx[j])
            yh = yl = None
            for row0, nseq, seq_len in groups:
                k_two = _hyena_filter(seq_len, hyena_filt_w_in[j], hyena_filt_b[j], hyena_filt_w_hid[j],
                                      hyena_filt_freq[j], hyena_filt_w_out[j])
                yh = _hyena_long_conv_group(u, x0, k_two, hyena_bias[j], row0, nseq, seq_len, yh)
                yl = _rg_lru_group(p, 3 * hw, 3 * hw + wa.shape[1] * wa.shape[2], lru_conv_w[j], lru_conv_b[j],
                                   wa, lru_ba[j], wx, lru_bx[j], lru_lambda[j], row0, nseq, seq_len, yl)
            x = _proj_residual(x, [yh, yl], bf(even_w_out[j]))
        else:
            lambda_init = 0.8 - 0.6 * math.exp(-0.3 * i)
            qkv = _qkv(x, mix_norm[i], bf(attn_w_qkv[j]), rope, groups, head_dim=head_dim)
            o = None
            for row0, nseq, seq_len in groups:
                o = _diff_attention_group(qkv, attn_lambda[j], attn_subln[j], row0, nseq, seq_len, lambda_init, o)
            x = _proj_residual(x, [o], bf(attn_w_o[j]))
        x = _ffn(x, ffn_norm[i, 1], bf(ffn_w1[i, 1]), bf(ffn_w3[i, 1]), bf(ffn_w2[i, 1]))

    y = _final_norm(x, final_norm)
    outs = []
    for (row0, nseq, seq_len), xg in zip(groups, (x_prompt, x_sample)):
        outs.append(y[row0:row0 + nseq * seq_len].reshape(xg.shape))
    return tuple(outs)
```

```python
import functools
import math

import numpy as np
import jax
import jax.numpy as jnp
from jax import lax
from jax.experimental import pallas as pl
from jax.experimental.pallas import tpu as pltpu

F32 = jnp.float32
BF16 = jnp.bfloat16

NORM_EPS = 1e-6
HYENA_SHORT = 3
HYENA_EMB = 33
HYENA_BANDS = (HYENA_EMB - 1) // 2
HYENA_INNER = 2
HYENA_TARGET = 1e-2
HYENA_FAST = 0.3
HYENA_SLOW = 1.5
LRU_BLOCKS = 8
LRU_CONV = 4
LRU_C = 8.0
ATT_HEADS = 16
ROPE_THETA = 10000.0
LOG2E = 1.4426950408889634

LANES = 128
SUBLANES = 8
VMEM_LIMIT_BYTES = 56 * 1024 * 1024
FFT_INNER = 128


def _params(*sem):
    return pltpu.CompilerParams(dimension_semantics=sem, vmem_limit_bytes=VMEM_LIMIT_BYTES)


def _rms(x, g):
    ms = jnp.mean(x * x, axis=-1, keepdims=True)
    return x * lax.rsqrt(ms + NORM_EPS) * g


def _alias_prev(in_specs, args, prev):
    if prev is None:
        return {}
    in_specs.append(pl.BlockSpec(memory_space=pl.ANY))
    args.append(prev)
    return {len(args) - 1: 0}


def _ffn_kernel(x_ref, g_ref, w1_ref, w3_ref, w2_ref, *rest, final_norm):
    gf_ref = rest[0] if final_norm else None
    o_ref, xn_ref = rest[-2:]
    f = pl.program_id(1)

    @pl.when(f == 0)
    def _():
        x = x_ref[...]
        xn_ref[...] = _rms(x, g_ref[...]).astype(BF16)
        o_ref[...] = x

    xn = xn_ref[...]
    a = jnp.dot(xn, w1_ref[...], preferred_element_type=F32)
    b = jnp.dot(xn, w3_ref[...], preferred_element_type=F32)
    h = (0.5 * a * jax.nn.sigmoid(a) * b).astype(BF16)
    o_ref[...] += jnp.dot(h, w2_ref[...], preferred_element_type=F32)

    if final_norm:
        @pl.when(f == pl.num_programs(1) - 1)
        def _():
            o_ref[...] = _rms(o_ref[...], gf_ref[...])


def _ffn(x, g, w1, w3, w2, *, in_row0=0, nrows=None, out_rows=None, out_row0=0, prev=None, final_g=None,
         tm=1024, tf=512):
    d = x.shape[1]
    mx = x.shape[0] if nrows is None else nrows
    ff = w1.shape[1]
    tm = min(tm, mx)
    tf = min(tf, ff)
    out_rows = mx if out_rows is None else out_rows
    rb0 = out_row0 // tm
    ib0 = in_row0 // tm
    in_specs = [
        pl.BlockSpec((tm, d), lambda i, f: (ib0 + i, 0)),
        pl.BlockSpec((1, d), lambda i, f: (0, 0)),
        pl.BlockSpec((d, tf), lambda i, f: (0, f)),
        pl.BlockSpec((d, tf), lambda i, f: (0, f)),
        pl.BlockSpec((tf, d), lambda i, f: (f, 0)),
    ]
    args = [x, g.reshape(1, d), w1, w3, w2]
    if final_g is not None:
        in_specs.append(pl.BlockSpec((1, d), lambda i, f: (0, 0)))
        args.append(final_g.reshape(1, d))
    aliases = _alias_prev(in_specs, args, prev)
    return pl.pallas_call(
        functools.partial(_ffn_kernel, final_norm=final_g is not None),
        grid=(mx // tm, ff // tf),
        in_specs=in_specs,
        out_specs=pl.BlockSpec((tm, d), lambda i, f: (rb0 + i, 0)),
        out_shape=jax.ShapeDtypeStruct((out_rows, d), F32),
        input_output_aliases=aliases,
        scratch_shapes=[pltpu.VMEM((tm, d), BF16)],
        compiler_params=_params("parallel", "arbitrary"),
        name="ffn",
    )(*args)


def _norm_matmul_kernel(x_ref, g_ref, w_ref, o_ref, xn_ref):
    @pl.when(pl.program_id(1) == 0)
    def _():
        xn_ref[...] = _rms(x_ref[...], g_ref[...]).astype(BF16)

    o_ref[...] = jnp.dot(xn_ref[...], w_ref[...], preferred_element_type=F32).astype(o_ref.dtype)


def _norm_matmul(x, g, w, *, out_dtype=F32, tm=1024, tn=512):
    m, d = x.shape
    n = w.shape[1]
    tm = min(tm, m)
    tn = min(tn, n)
    return pl.pallas_call(
        _norm_matmul_kernel,
        grid=(m // tm, n // tn),
        in_specs=[
            pl.BlockSpec((tm, d), lambda i, j: (i, 0)),
            pl.BlockSpec((1, d), lambda i, j: (0, 0)),
            pl.BlockSpec((d, tn), lambda i, j: (0, j)),
        ],
        out_specs=pl.BlockSpec((tm, tn), lambda i, j: (i, j)),
        out_shape=jax.ShapeDtypeStruct((m, n), out_dtype),
        scratch_shapes=[pltpu.VMEM((tm, d), BF16)],
        compiler_params=_params("parallel", "arbitrary"),
        name="norm_matmul",
    )(x, g.reshape(1, d), w)


def _rope_tables(lmax, head_dim):
    half = head_dim // 2
    inv = ROPE_THETA ** (-np.arange(0, head_dim, 2, dtype=np.float64) / head_dim)
    ang = np.arange(lmax, dtype=np.float64)[:, None] * inv[None, :]
    ang = np.concatenate([ang, ang], axis=-1)
    cos = np.tile(np.cos(ang), (1, LANES // head_dim))
    sin = np.tile(np.sin(ang), (1, LANES // head_dim))
    first = (np.arange(LANES) % head_dim) < half
    sin_up = np.where(first[None, :], -sin, 0.0)
    sin_dn = np.where(first[None, :], 0.0, sin)
    q_scale = head_dim ** -0.5 * LOG2E
    one, zero = np.ones_like(cos), np.zeros_like(cos)
    stack = lambda q, k, v: jnp.asarray(np.stack([q, k, v]), F32)
    return (stack(cos * q_scale, cos, one), stack(sin_up * q_scale, sin_up, zero),
            stack(sin_dn * q_scale, sin_dn, zero))


def _qkv_kernel(x_ref, g_ref, w_ref, cos_ref, sup_ref, sdn_ref, o_ref, xn_ref, *, half):
    @pl.when(pl.program_id(1) == 0)
    def _():
        xn_ref[...] = _rms(x_ref[...], g_ref[...]).astype(BF16)

    res = jnp.dot(xn_ref[...], w_ref[...], preferred_element_type=F32)
    cos, sup, sdn = cos_ref[...], sup_ref[...], sdn_ref[...]
    for c in range(res.shape[1] // LANES):
        blk = res[:, c * LANES:(c + 1) * LANES]
        rot = blk * cos + pltpu.roll(blk, LANES - half, 1) * sup + pltpu.roll(blk, half, 1) * sdn
        o_ref[:, c * LANES:(c + 1) * LANES] = rot.astype(o_ref.dtype)


def _qkv(x, g, w, tables, groups, *, head_dim, tm=1024, tn=512):
    m, d = x.shape
    n = w.shape[1]
    tm = min(tm, min(l for _, _, l in groups))
    n_sec = d // tn
    cos, sup, sdn = tables

    def table_block(i, j):
        blk = 0
        for row0, nseq, l in groups:
            inside = jnp.logical_and(i * tm >= row0, i * tm < row0 + nseq * l)
            blk = jnp.where(inside, ((i * tm - row0) % l) // tm, blk)
        return (j // n_sec, blk, 0)

    table_spec = pl.BlockSpec((None, tm, LANES), table_block)
    return pl.pallas_call(
        functools.partial(_qkv_kernel, half=head_dim // 2),
        grid=(m // tm, n // tn),
        in_specs=[
            pl.BlockSpec((tm, d), lambda i, j: (i, 0)),
            pl.BlockSpec((1, d), lambda i, j: (0, 0)),
            pl.BlockSpec((d, tn), lambda i, j: (0, j)),
            table_spec, table_spec, table_spec,
        ],
        out_specs=pl.BlockSpec((tm, tn), lambda i, j: (i, j)),
        out_shape=jax.ShapeDtypeStruct((m, n), BF16),
        scratch_shapes=[pltpu.VMEM((tm, d), BF16)],
        compiler_params=_params("parallel", "arbitrary"),
        name="qkv_rope",
    )(x, g.reshape(1, d), w, cos, sup, sdn)


def _attn_kernel(lam_ref, g_ref, q_ref, k_ref, v_ref, *rest, tk, tq, lambda_init):
    o_ref, vt_ref, acc_ref, sa_ref, sb_ref = rest[-5:]
    seq_len = k_ref.shape[0]
    nchunk = seq_len // tk
    nsub = q_ref.shape[0] // tq
    half = q_ref.shape[1] // 2

    @pl.when(pl.program_id(2) == 0)
    def _():
        for c in range(nchunk):
            vt_ref[c] = v_ref[c * tk:(c + 1) * tk, :].astype(F32).T.astype(BF16)

    lane = lax.broadcasted_iota(jnp.int32, (tq, q_ref.shape[1]), 1)
    lv = lam_ref[...]
    lam = (jnp.exp(jnp.sum(lv[0:1] * lv[1:2], axis=1, keepdims=True))
           - jnp.exp(jnp.sum(lv[2:3] * lv[3:4], axis=1, keepdims=True)) + lambda_init)

    def rows(j):
        return pl.ds(j * tq if isinstance(j, int) else pl.multiple_of(j * tq, tq), tq)

    def q_parts(j):
        q = q_ref[rows(j), :]
        zero = jnp.zeros_like(q)
        return jnp.where(lane < half, q, zero), jnp.where(lane >= half, q, zero)

    def score_step(c, ms, qs, s_out):
        k = k_ref[pl.ds(pl.multiple_of(c * tk, tk), tk), :]
        new = []
        for comp in range(2):
            s = lax.dot_general(k, qs[comp], (((1,), (1,)), ((), ())), preferred_element_type=F32)
            s_out[comp, c] = s
            new.append(jnp.maximum(ms[comp], jnp.max(s, axis=0, keepdims=True)))
        return tuple(new)

    def pv_step(c, ls, ms, s_in):
        vt = vt_ref[c]
        new = []
        for comp in range(2):
            p = jnp.exp2(s_in[comp, c] - ms[comp])
            new.append(ls[comp] + jnp.sum(p, axis=0, keepdims=True))
            acc_ref[comp] += jnp.dot(vt, p.astype(BF16), preferred_element_type=F32)
        return tuple(new)

    def finalize(j, ls):
        ot = acc_ref[0] * (1.0 / ls[0]) - acc_ref[1] * (lam * (1.0 / ls[1]))
        ms2 = jnp.mean(ot * ot, axis=0, keepdims=True)
        ot = ot * lax.rsqrt(ms2 + NORM_EPS) * g_ref[...] * (1.0 - lambda_init)
        o_ref[rows(j), :] = ot.T.astype(o_ref.dtype)

    neg = (jnp.full((1, tq), -jnp.inf, F32),) * 2
    zero_rows = (jnp.zeros((1, tq), F32),) * 2

    qs0 = q_parts(0)
    ms0 = lax.fori_loop(0, nchunk, lambda c, m: score_step(c, m, qs0, sa_ref), neg, unroll=min(4, nchunk))

    def fused(j, ms, s_in, s_out):
        acc_ref[...] = jnp.zeros_like(acc_ref)
        qn = q_parts(j + 1)

        def both(c, carry):
            ls, mn = carry
            return pv_step(c, ls, ms, s_in), score_step(c, mn, qn, s_out)

        ls, ms_next = lax.fori_loop(0, nchunk, both, (zero_rows, neg), unroll=min(8, nchunk))
        finalize(j, ls)
        return ms_next

    def pair(jj, ms):
        return fused(2 * jj + 1, fused(2 * jj, ms, sa_ref, sb_ref), sb_ref, sa_ref)

    ms_last = lax.fori_loop(0, (nsub - 1) // 2, pair, ms0)
    if (nsub - 1) % 2:
        ms_last = fused(nsub - 2, ms_last, sa_ref, sb_ref)
    last = nsub - 1
    s_last = sb_ref if last % 2 else sa_ref
    acc_ref[...] = jnp.zeros_like(acc_ref)
    ls = lax.fori_loop(0, nchunk, lambda c, l: pv_step(c, l, ms_last, s_last), zero_rows, unroll=min(4, nchunk))
    finalize(last, ls)


def _diff_attention_group(qkv, lam_vec, subln_g, row0, nseq, seq_len, lambda_init, prev, *, tq=256, nsub=16,
                          tk=512):
    m = qkv.shape[0]
    d = qkv.shape[1] // 3
    hw = d // ATT_HEADS
    tq = min(tq, seq_len)
    tk = min(tk, seq_len)
    bq = min(nsub * tq, seq_len)
    nq = seq_len // bq
    qb0 = row0 // bq
    kb0 = row0 // seq_len
    in_specs = [
        pl.BlockSpec(lam_vec.shape, lambda b, h, i: (0, 0)),
        pl.BlockSpec((hw, 1), lambda b, h, i: (0, 0)),
        pl.BlockSpec((bq, hw), lambda b, h, i: (qb0 + b * nq + i, h)),
        pl.BlockSpec((seq_len, hw), lambda b, h, i: (kb0 + b, ATT_HEADS + h)),
        pl.BlockSpec((seq_len, hw), lambda b, h, i: (kb0 + b, 2 * ATT_HEADS + h)),
    ]
    args = [lam_vec, subln_g.reshape(hw, 1), qkv, qkv, qkv]
    aliases = _alias_prev(in_specs, args, prev)
    return pl.pallas_call(
        functools.partial(_attn_kernel, tk=tk, tq=tq, lambda_init=lambda_init),
        grid=(nseq, ATT_HEADS, nq),
        in_specs=in_specs,
        out_specs=pl.BlockSpec((bq, hw), lambda b, h, i: (qb0 + b * nq + i, h)),
        out_shape=jax.ShapeDtypeStruct((m, d), BF16),
        input_output_aliases=aliases,
        scratch_shapes=[
            pltpu.VMEM((seq_len // tk, hw, tk), BF16),
            pltpu.VMEM((2, hw, tq), F32),
            pltpu.VMEM((2, seq_len // tk, tk, tq), F32),
            pltpu.VMEM((2, seq_len // tk, tk, tq), F32),
        ],
        compiler_params=_params("parallel", "parallel", "arbitrary"),
        name="diff_attention",
    )(*args)


def _proj_res_kernel(*refs, nparts):
    res_ref = refs[0]
    a_refs = refs[1:1 + nparts]
    w_refs = refs[1 + nparts:1 + 2 * nparts]
    o_ref = refs[1 + 2 * nparts]
    acc = res_ref[...]
    for a_ref, w_ref in zip(a_refs, w_refs):
        acc = acc + jnp.dot(a_ref[...].astype(BF16), w_ref[...], preferred_element_type=F32)
    o_ref[...] = acc


def _proj_residual(res, parts, w, *, tm=512, tn=2048):
    m, n = res.shape
    tm = min(tm, m)
    tn = min(tn, n)
    nparts = len(parts)
    kp = w.shape[0] // nparts
    in_specs = [pl.BlockSpec((tm, tn), lambda i, j: (i, j))]
    in_specs += [pl.BlockSpec((tm, kp), lambda i, j: (i, 0)) for _ in parts]
    in_specs += [pl.BlockSpec((kp, tn), functools.partial(lambda i, j, p: (p, j), p=p)) for p in range(nparts)]
    return pl.pallas_call(
        functools.partial(_proj_res_kernel, nparts=nparts),
        grid=(m // tm, n // tn),
        in_specs=in_specs,
        out_specs=pl.BlockSpec((tm, tn), lambda i, j: (i, j)),
        out_shape=jax.ShapeDtypeStruct((m, n), F32),
        compiler_params=_params("parallel", "arbitrary"),
        name="proj_residual",
    )(res, *parts, *([w] * nparts))


def _is_any(i, values):
    hit = i == values[0]
    for v in values[1:]:
        hit = jnp.logical_or(hit, i == v)
    return hit


def _hyena_pre_kernel(*refs, tt, first_tiles, last_tiles):
    (x0m, x0p, x0n, x1m, x1p, x1n, vm, vp, vn, w0, w1, w2, b0, b1, b2, x0_out, u_out) = refs
    i = pl.program_id(0)
    first = _is_any(i, first_tiles)
    last = _is_any(i, last_tiles)

    def conv(m_ref, p_ref, n_ref, w_ref, b_ref):
        x = m_ref[...]
        row = lax.broadcasted_iota(jnp.int32, x.shape, 0)
        prev_row = jnp.where(first, 0.0, p_ref[SUBLANES - 1:SUBLANES, :])
        next_row = jnp.where(last, 0.0, n_ref[0:1, :])
        xm = jnp.where(row == 0, prev_row, pltpu.roll(x, 1, 0))
        xp = jnp.where(row == tt - 1, next_row, pltpu.roll(x, tt - 1, 0))
        return b_ref[...] + xm * w_ref[0:1, :] + x * w_ref[1:2, :] + xp * w_ref[2:3, :]

    x0_out[...] = conv(x0m, x0p, x0n, w0, b0)
    u_out[...] = conv(vm, vp, vn, w2, b2) * conv(x1m, x1p, x1n, w1, b1)


def _hyena_pre(p, conv_w, conv_b, groups, width, *, tt=512, ct=512):
    m = p.shape[0]
    tt = min(tt, min(l for _, _, l in groups))
    ct = min(ct, width)
    nc = width // ct
    first_tiles = tuple((row0 + s * l) // tt for row0, nseq, l in groups for s in range(nseq))
    last_tiles = tuple((row0 + (s + 1) * l) // tt - 1 for row0, nseq, l in groups for s in range(nseq))
    hb = tt // SUBLANES
    nhb = m // SUBLANES

    def main(sec):
        return pl.BlockSpec((tt, ct), lambda i, j: (i, sec * nc + j))

    def prev(sec):
        return pl.BlockSpec((SUBLANES, ct), lambda i, j: (jnp.maximum(i * hb - 1, 0), sec * nc + j))

    def nxt(sec):
        return pl.BlockSpec((SUBLANES, ct), lambda i, j: (jnp.minimum((i + 1) * hb, nhb - 1), sec * nc + j))

    def wspec(sec):
        return pl.BlockSpec((HYENA_SHORT, ct), lambda i, j: (0, sec * nc + j))

    def bspec(sec):
        return pl.BlockSpec((1, ct), lambda i, j: (0, sec * nc + j))

    in_specs = []
    for sec in range(3):
        in_specs += [main(sec), prev(sec), nxt(sec)]
    in_specs += [wspec(0), wspec(1), wspec(2), bspec(0), bspec(1), bspec(2)]
    out_spec = pl.BlockSpec((tt, ct), lambda i, j: (i, j))
    cb = conv_b.reshape(1, -1)
    return pl.pallas_call(
        functools.partial(_hyena_pre_kernel, tt=tt, first_tiles=first_tiles, last_tiles=last_tiles),
        grid=(m // tt, nc),
        in_specs=in_specs,
        out_specs=[out_spec, out_spec],
        out_shape=[jax.ShapeDtypeStruct((m, width), F32)] * 2,
        compiler_params=_params("parallel", "arbitrary"),
        name="hyena_short_conv",
    )(*([p] * 9), conv_w, conv_w, conv_w, cb, cb, cb)


def _hyena_positions(seq_len):
    pos = np.arange(seq_len, dtype=np.float64)[:, None]
    t = pos / max(seq_len - 1, 1)
    w = (2.0 * math.pi / seq_len) * pos
    bands = np.linspace(1e-4, HYENA_BANDS - 1, HYENA_BANDS, dtype=np.float64)[None, :]
    z = np.concatenate([t, np.cos(bands * w), -np.sin(bands * w)], axis=-1)
    z2 = np.concatenate([z, z[:1], z[:0:-1]], axis=0)
    out = np.zeros((2 * seq_len, 64), np.float32)
    out[:, :HYENA_EMB] = z2
    return jnp.asarray(out)


def _filter_kernel(z_ref, win_ref, b_ref, whid_ref, fr_ref, wout_ref, dl_ref, o_ref, *, seq_len):
    tr = z_ref.shape[0]
    hp = lax.Precision.HIGHEST
    z = z_ref[...]
    fr = fr_ref[...]
    h = jnp.sin(fr * (jnp.dot(z, win_ref[...], precision=hp, preferred_element_type=F32) + b_ref[0:1, :]))
    for j in range(HYENA_INNER):
        h = jnp.sin(fr * (jnp.dot(h, whid_ref[j], precision=hp, preferred_element_type=F32)
                          + b_ref[j + 1:j + 2, :]))
    k = jnp.dot(h, wout_ref[...], precision=hp, preferred_element_type=F32)
    k = k * jnp.exp(-z[:, 0:1] * dl_ref[...])
    row = pl.program_id(0) * tr + lax.broadcasted_iota(jnp.int32, k.shape, 0)
    o_ref[...] = jnp.where(row == seq_len, 0.0, k)


def _hyena_filter(seq_len, w_in, b, w_hid, freq, w_out, *, tr=512):
    order = w_in.shape[1]
    width = w_out.shape[1] // 2
    tr = min(tr, seq_len)
    nfwd = seq_len // tr
    z2 = _hyena_positions(seq_len)
    w_in_p = jnp.zeros((z2.shape[1], order), F32).at[:HYENA_EMB].set(w_in)
    deltas = np.abs(np.linspace(math.log(HYENA_TARGET) / HYENA_SLOW, math.log(HYENA_TARGET) / HYENA_FAST,
                                width, dtype=np.float64))[None, :].astype(np.float32)
    return pl.pallas_call(
        functools.partial(_filter_kernel, seq_len=seq_len),
        grid=(2 * nfwd,),
        in_specs=[
            pl.BlockSpec((tr, z2.shape[1]), lambda i: (i, 0)),
            pl.BlockSpec(w_in_p.shape, lambda i: (0, 0)),
            pl.BlockSpec(b.shape, lambda i: (0, 0)),
            pl.BlockSpec(w_hid.shape, lambda i: (0, 0, 0)),
            pl.BlockSpec((1, order), lambda i: (0, 0)),
            pl.BlockSpec((order, width), lambda i: (0, i // nfwd)),
            pl.BlockSpec((1, width), lambda i: (0, 0)),
        ],
        out_specs=pl.BlockSpec((tr, width), lambda i: (i, 0)),
        out_shape=jax.ShapeDtypeStruct((2 * seq_len, width), F32),
        compiler_params=_params("parallel"),
        name="hyena_filter",
    )(z2, w_in_p, b, w_hid, freq.reshape(1, order), w_out, jnp.asarray(deltas))


def _dft_constants(seq_len):
    n = 2 * seq_len
    n2 = FFT_INNER
    n1 = n // n2
    a1 = 2.0 * np.pi * np.outer(np.arange(n1), np.arange(n1)) / n1
    f1 = np.concatenate([np.cos(a1), -np.sin(a1)], axis=0)
    f3 = np.concatenate([np.cos(a1), -np.sin(a1)], axis=1)[:n1 // 2]
    a2 = 2.0 * np.pi * np.outer(np.arange(n2), np.arange(n2)) / n2
    gr, gi = np.cos(a2), -np.sin(a2)
    g_fwd = np.block([[gr, -gi], [gi, gr]])
    g_inv = np.block([[gr, gi], [-gi, gr]])
    at = 2.0 * np.pi * np.outer(np.arange(n1), np.arange(n2)) / n
    tw_r = np.cos(at)[:, :, None]
    tw_i = -np.sin(at)[:, :, None]
    c = lambda a, dt: jnp.asarray(a, dt)
    return dict(n1=n1, n2=n2, f1_full=c(f1, BF16), f1_half=c(f1[:, :n1 // 2], BF16), f3=c(f3, BF16),
                g_fwd=c(g_fwd, BF16), g_inv=c(g_inv, BF16), tw_r=c(tw_r, F32), tw_i=c(tw_i, F32))


def _dft_outer_fwd_kernel(f_ref, x_ref, o_ref):
    f = f_ref[...]
    for s in range(x_ref.shape[1]):
        o_ref[:, s, :] = jnp.dot(f, x_ref[:, s, :].astype(BF16), preferred_element_type=F32)


def _dft_outer_forward(fmat, x, seq0, nseq, *, ct=1024):
    _, k, n2, c = x.shape
    mo = fmat.shape[0]
    ct = min(ct, c)
    return pl.pallas_call(
        _dft_outer_fwd_kernel,
        grid=(nseq, n2 // SUBLANES, c // ct),
        in_specs=[
            pl.BlockSpec((mo, k), lambda b, s, j: (0, 0)),
            pl.BlockSpec((None, k, SUBLANES, ct), lambda b, s, j: (seq0 + b, 0, s, j)),
        ],
        out_specs=pl.BlockSpec((None, mo, SUBLANES, ct), lambda b, s, j: (b, 0, s, j)),
        out_shape=jax.ShapeDtypeStruct((nseq, mo, n2, c), F32),
        compiler_params=_params("parallel", "parallel", "parallel"),
        name="dft_outer_forward",
    )(fmat, x)


def _twiddle_dft(ar_ref, ai_ref, twr_ref, twi_ref, g_ref):
    n2 = ar_ref.shape[0]
    ar, ai = ar_ref[...], ai_ref[...]
    tr, ti = twr_ref[...], twi_ref[...]
    z = jnp.concatenate([ar * tr - ai * ti, ar * ti + ai * tr], axis=0).astype(BF16)
    x = jnp.dot(g_ref[...], z, preferred_element_type=F32)
    return x[:n2], x[n2:]


def _spectrum_kernel(ar_ref, ai_ref, twr_ref, twi_ref, g_ref, kr_ref, ki_ref, *, scale):
    xr, xi = _twiddle_dft(ar_ref, ai_ref, twr_ref, twi_ref, g_ref)
    kr_ref[...] = xr * scale
    ki_ref[...] = xi * scale


def _fft_mid_kernel(ar_ref, ai_ref, twr_ref, twi_ref, g_ref, gi_ref, kr_ref, ki_ref, br_ref, bi_ref):
    n2 = ar_ref.shape[0]
    xr, xi = _twiddle_dft(ar_ref, ai_ref, twr_ref, twi_ref, g_ref)
    kr, ki = kr_ref[...], ki_ref[...]
    y = jnp.concatenate([xr * kr - xi * ki, xr * ki + xi * kr], axis=0).astype(BF16)
    b = jnp.dot(gi_ref[...], y, preferred_element_type=F32)
    br, bi = b[:n2], b[n2:]
    tr, ti = twr_ref[...], twi_ref[...]
    br_ref[...] = br * tr + bi * ti
    bi_ref[...] = bi * tr - br * ti


def _fft_mid(a, consts, kf=None, *, ct=1024):
    nb, _, n2, c = a.shape
    n1 = consts["n1"]
    ct = min(ct, c)
    grid = (nb, n1, c // ct)
    blk = lambda im: pl.BlockSpec((None, None, n2, ct), lambda b, k, j: (b, im * n1 + k, 0, j))
    tw = pl.BlockSpec((None, n2, 1), lambda b, k, j: (k, 0, 0))
    gs = pl.BlockSpec((2 * n2, 2 * n2), lambda b, k, j: (0, 0))
    half_spec = pl.BlockSpec((None, None, n2, ct), lambda b, k, j: (b, k, 0, j))
    half_shape = jax.ShapeDtypeStruct((nb, n1, n2, c), F32)
    if kf is None:
        return pl.pallas_call(
            functools.partial(_spectrum_kernel, scale=1.0 / (n1 * n2)),
            grid=grid,
            in_specs=[blk(0), blk(1), tw, tw, gs],
            out_specs=[half_spec, half_spec],
            out_shape=[half_shape, half_shape],
            compiler_params=_params("parallel", "parallel", "parallel"),
            name="dft_filter_spectrum",
        )(a, a, consts["tw_r"], consts["tw_i"], consts["g_fwd"])
    kblk = lambda: pl.BlockSpec((None, None, n2, ct), lambda b, k, j: (0, k, 0, j))
    return pl.pallas_call(
        _fft_mid_kernel,
        grid=grid,
        in_specs=[blk(0), blk(1), tw, tw, gs, gs, kblk(), kblk()],
        out_specs=[half_spec, half_spec],
        out_shape=[half_shape, half_shape],
        compiler_params=_params("parallel", "parallel", "parallel"),
        name="dft_inner_filter",
    )(a, a, consts["tw_r"], consts["tw_i"], consts["g_fwd"], consts["g_inv"], kf[0], kf[1])


def _dft_outer_inv_kernel(f_ref, br_ref, bi_ref, u_ref, x0_ref, bias_ref, *rest):
    o_ref = rest[-1]
    f = f_ref[...]
    bias = bias_ref[...]
    for s in range(u_ref.shape[1]):
        b = jnp.concatenate([br_ref[:, s, :], bi_ref[:, s, :]], axis=0).astype(BF16)
        y = jnp.dot(f, b, preferred_element_type=F32)
        o_ref[:, s, :] = x0_ref[:, s, :] * (y + u_ref[:, s, :] * bias)


def _dft_outer_inverse(f3, br, bi, u, x0, bias, seq0, prev, *, ct=1024):
    nseq, n1, n2, c = br.shape
    mo = f3.shape[0]
    ct = min(ct, c)
    spec_in = pl.BlockSpec((None, n1, SUBLANES, ct), lambda b, s, j: (b, 0, s, j))
    spec_tok = pl.BlockSpec((None, mo, SUBLANES, ct), lambda b, s, j: (seq0 + b, 0, s, j))
    in_specs = [pl.BlockSpec(f3.shape, lambda b, s, j: (0, 0)), spec_in, spec_in, spec_tok, spec_tok,
                pl.BlockSpec((1, ct), lambda b, s, j: (0, j))]
    args = [f3, br, bi, u, x0, bias.reshape(1, c)]
    aliases = _alias_prev(in_specs, args, None if prev is None else prev.reshape(u.shape))
    return pl.pallas_call(
        _dft_outer_inv_kernel,
        grid=(nseq, n2 // SUBLANES, c // ct),
        in_specs=in_specs,
        out_specs=spec_tok,
        out_shape=jax.ShapeDtypeStruct(u.shape, F32),
        input_output_aliases=aliases,
        compiler_params=_params("parallel", "parallel", "parallel"),
        name="dft_outer_inverse",
    )(*args)


def _hyena_long_conv_group(u, x0, k_two, bias, row0, nseq, seq_len, prev):
    m, c = u.shape
    consts = _dft_constants(seq_len)
    n1, n2 = consts["n1"], consts["n2"]
    ak = _dft_outer_forward(consts["f1_full"], k_two.reshape(1, n1, n2, c), 0, 1)
    kfr, kfi = _fft_mid(ak, consts)
    tok_view = (m // seq_len, n1 // 2, n2, c)
    seq0 = row0 // seq_len
    a = _dft_outer_forward(consts["f1_half"], u.reshape(tok_view), seq0, nseq)
    br, bi = _fft_mid(a, consts, kf=(kfr, kfi))
    y = _dft_outer_inverse(consts["f3"], br, bi, u.reshape(tok_view), x0.reshape(tok_view), bias, seq0, prev)
    return y.reshape(m, c)


def _softplus(z):
    return jnp.maximum(z, 0.0) + jnp.log(1.0 + jnp.exp(-jnp.abs(z)))


def _lru_kernel(pl_ref, pg_ref, cw_ref, cb_ref, wa_ref, ba_ref, wx_ref, bx_ref, lam_ref, *rest, tchunk):
    o_ref, xp_ref, hs_ref = rest[-3:]
    seq_len, cw = pl_ref.shape
    nchunk = seq_len // tchunk
    halo = SUBLANES
    xp_ref[0:halo, :] = jnp.zeros((halo, cw), F32)
    xp_ref[halo + seq_len:2 * halo + seq_len, :] = jnp.zeros((halo, cw), F32)
    xp_ref[halo:halo + seq_len, :] = pl_ref[...]

    row = lax.broadcasted_iota(jnp.int32, (tchunk, cw), 0)
    nsteps = int(math.log2(tchunk))
    neg_c_softplus = -LRU_C * _softplus(-lam_ref[...])

    def gates(c, direction):
        base = pl.multiple_of(c * tchunk, tchunk)
        xw = xp_ref[pl.ds(base, tchunk + 2 * halo), :]
        ext = tchunk + 2 * halo
        xb = cb_ref[...]
        for j in range(LRU_CONV):
            shift = (LRU_CONV // 2 - j) % ext
            tap = xw if shift == 0 else pltpu.roll(xw, shift, 0)
            xb = xb + tap[halo:halo + tchunk, :] * cw_ref[j:j + 1, :]
        xb16 = xb.astype(BF16)
        r = jax.nn.sigmoid(jnp.dot(xb16, wa_ref[direction], preferred_element_type=F32)
                           + ba_ref[direction:direction + 1, :])
        i = jax.nn.sigmoid(jnp.dot(xb16, wx_ref[direction], preferred_element_type=F32)
                           + bx_ref[direction:direction + 1, :])
        log_a = r * neg_c_softplus[direction:direction + 1, :]
        a = jnp.exp(log_a)
        gain = jnp.sqrt(-jnp.tanh(log_a) * (a * a + 1.0))
        return a, gain * (i * xb)

    def scan(a, b, reverse):
        for s in range(nsteps):
            d = 1 << s
            if reverse:
                keep = row < tchunk - d
                a_sh = pltpu.roll(a, tchunk - d, 0)
                b_sh = pltpu.roll(b, tchunk - d, 0)
            else:
                keep = row >= d
                a_sh = pltpu.roll(a, d, 0)
                b_sh = pltpu.roll(b, d, 0)
            b = jnp.where(keep, a * b_sh + b, b)
            a = jnp.where(keep, a * a_sh, a)
        return a, b

    def fwd_body(c, h):
        a, b = gates(c, 0)
        a, b = scan(a, b, False)
        hc = a * h + b
        hs_ref[pl.ds(pl.multiple_of(c * tchunk, tchunk), tchunk), :] = hc
        return hc[tchunk - 1:tchunk, :]

    lax.fori_loop(0, nchunk, fwd_body, jnp.zeros((1, cw), F32), unroll=min(4, nchunk))

    def rev_body(cc, h):
        c = nchunk - 1 - cc
        a, b = gates(c, 1)
        a, b = scan(a, b, True)
        hc = a * h + b
        sl = pl.ds(pl.multiple_of(c * tchunk, tchunk), tchunk)
        o_ref[sl, :] = (hs_ref[sl, :] + hc) * jax.nn.gelu(pg_ref[sl, :])
        return hc[0:1, :]

    lax.fori_loop(0, nchunk, rev_body, jnp.zeros((1, cw), F32), unroll=min(4, nchunk))


def _rg_lru_group(p, col_gate, col_x, conv_w, conv_b, wa, ba, wx, bx, lam, row0, nseq, seq_len, prev, *,
                  tchunk=128):
    m = p.shape[0]
    nblk = wa.shape[1]
    cw = wa.shape[2]
    tchunk = min(tchunk, seq_len)
    sb0 = row0 // seq_len
    gb0 = col_gate // cw
    xb0 = col_x // cw
    vec = lambda rows: pl.BlockSpec((rows, cw), lambda b, n: (0, n))
    mat = pl.BlockSpec((2, None, cw, cw), lambda b, n: (0, n, 0, 0))
    in_specs = [
        pl.BlockSpec((seq_len, cw), lambda b, n: (sb0 + b, xb0 + n)),
        pl.BlockSpec((seq_len, cw), lambda b, n: (sb0 + b, gb0 + n)),
        vec(LRU_CONV), vec(1), mat, vec(2), mat, vec(2), vec(2),
    ]
    args = [p, p, conv_w, conv_b.reshape(1, -1), wa, ba, wx, bx, lam]
    aliases = _alias_prev(in_specs, args, prev)
    return pl.pallas_call(
        functools.partial(_lru_kernel, tchunk=tchunk),
        grid=(nseq, nblk),
        in_specs=in_specs,
        out_specs=pl.BlockSpec((seq_len, cw), lambda b, n: (sb0 + b, n)),
        out_shape=jax.ShapeDtypeStruct((m, nblk * cw), F32),
        input_output_aliases=aliases,
        scratch_shapes=[pltpu.VMEM((seq_len + 2 * SUBLANES, cw), F32), pltpu.VMEM((seq_len, cw), F32)],
        compiler_params=_params("parallel", "parallel"),
        name="rg_lru",
    )(*args)


def kernel(x_prompt, x_sample, ffn_norm, ffn_w1, ffn_w3, ffn_w2, mix_norm, final_norm, even_w_in, hyena_conv_w, hyena_conv_b, hyena_filt_w_in, hyena_filt_b, hyena_filt_w_hid, hyena_filt_freq, hyena_filt_w_out, hyena_bias, lru_conv_w, lru_conv_b, lru_wa, lru_ba, lru_wx, lru_bx, lru_lambda, even_w_out, attn_w_qkv, attn_lambda, attn_subln, attn_w_o):
    d = x_prompt.shape[-1]
    depth = ffn_norm.shape[0]
    groups = []
    row0 = 0
    for xg in (x_prompt, x_sample):
        groups.append((row0, xg.shape[0], xg.shape[1]))
        row0 += xg.shape[0] * xg.shape[1]
    groups = tuple(groups)
    m = row0
    inputs = (x_prompt, x_sample)

    hw = hyena_bias.shape[1]
    head_dim = d // (2 * ATT_HEADS)
    rope = _rope_tables(max(l for _, _, l in groups), head_dim)
    bf = lambda w: w.astype(BF16)

    x = None
    for i in range(depth):
        j = i // 2
        w_first = (ffn_norm[i, 0], bf(ffn_w1[i, 0]), bf(ffn_w3[i, 0]), bf(ffn_w2[i, 0]))
        if i == 0:
            for (row0, _, _), xg in zip(groups, inputs):
                x = _ffn(xg.reshape(-1, d), *w_first, out_rows=m, out_row0=row0, prev=x)
        else:
            x = _ffn(x, *w_first)
        if i % 2 == 0:
            p = _norm_matmul(x, mix_norm[i], bf(even_w_in[j]))
            x0, u = _hyena_pre(p, hyena_conv_w[j], hyena_conv_b[j], groups, hw)
            wa, wx = bf(lru_wa[j]), bf(lru_wx[j])
            yh = yl = None
            for row0, nseq, seq_len in groups:
                k_two = _hyena_filter(seq_len, hyena_filt_w_in[j], hyena_filt_b[j], hyena_filt_w_hid[j],
                                      hyena_filt_freq[j], hyena_filt_w_out[j])
                yh = _hyena_long_conv_group(u, x0, k_two, hyena_bias[j], row0, nseq, seq_len, yh)
                yl = _rg_lru_group(p, 3 * hw, 3 * hw + wa.shape[1] * wa.shape[2], lru_conv_w[j], lru_conv_b[j],
                                   wa, lru_ba[j], wx, lru_bx[j], lru_lambda[j], row0, nseq, seq_len, yl)
            x = _proj_residual(x, [yh, yl], bf(even_w_out[j]))
        else:
            lambda_init = 0.8 - 0.6 * math.exp(-0.3 * i)
            qkv = _qkv(x, mix_norm[i], bf(attn_w_qkv[j]), rope, groups, head_dim=head_dim)
            o = None
            for row0, nseq, seq_len in groups:
                o = _diff_attention_group(qkv, attn_lambda[j], attn_subln[j], row0, nseq, seq_len, lambda_init, o)
            x = _proj_residual(x, [o], bf(attn_w_o[j]))
        w_second = (ffn_norm[i, 1], bf(ffn_w1[i, 1]), bf(ffn_w3[i, 1]), bf(ffn_w2[i, 1]))
        if i + 1 < depth:
            x = _ffn(x, *w_second)

    outs = []
    for (row0, nseq, seq_len), xg in zip(groups, inputs):
        y = _ffn(x, *w_second, in_row0=row0, nrows=nseq * seq_len, final_g=final_norm)
        outs.append(y.reshape(xg.shape))
    return tuple(outs)
```

```python
import functools
import math

import numpy as np
import jax
import jax.numpy as jnp
from jax import lax
from jax.experimental import pallas as pl
from jax.experimental.pallas import tpu as pltpu

F32 = jnp.float32
BF16 = jnp.bfloat16

NORM_EPS = 1e-6
HYENA_SHORT = 3
HYENA_EMB = 33
HYENA_BANDS = (HYENA_EMB - 1) // 2
HYENA_INNER = 2
HYENA_TARGET = 1e-2
HYENA_FAST = 0.3
HYENA_SLOW = 1.5
LRU_BLOCKS = 8
LRU_CONV = 4
LRU_C = 8.0
ATT_HEADS = 16
ROPE_THETA = 10000.0
LOG2E = 1.4426950408889634

LANES = 128
SUBLANES = 8
VMEM_LIMIT_BYTES = 56 * 1024 * 1024
FFT_INNER = 128


def _params(*sem):
    return pltpu.CompilerParams(dimension_semantics=sem, vmem_limit_bytes=VMEM_LIMIT_BYTES)


def _rms(x, g):
    ms = jnp.mean(x * x, axis=-1, keepdims=True)
    return x * lax.rsqrt(ms + NORM_EPS) * g


def _stacked_spec(lead, block, index):
    lead = tuple(lead)
    return pl.BlockSpec((None,) * len(lead) + tuple(block), lambda *g: lead + tuple(index(*g)))


def _alias_prev(in_specs, args, prev):
    if prev is None:
        return {}
    in_specs.append(pl.BlockSpec(memory_space=pl.ANY))
    args.append(prev)
    return {len(args) - 1: 0}


def _ffn_kernel(x_ref, g_ref, w1_ref, w3_ref, w2_ref, *rest, final_norm):
    gf_ref = rest[0] if final_norm else None
    o_ref, xn_ref = rest[-2:]
    f = pl.program_id(1)

    @pl.when(f == 0)
    def _():
        x = x_ref[...]
        xn_ref[...] = _rms(x, g_ref[...]).astype(BF16)
        o_ref[...] = x

    xn = xn_ref[...]
    a = jnp.dot(xn, w1_ref[...], preferred_element_type=F32)
    b = jnp.dot(xn, w3_ref[...], preferred_element_type=F32)
    h = (0.5 * a * jax.nn.sigmoid(a) * b).astype(BF16)
    o_ref[...] += jnp.dot(h, w2_ref[...], preferred_element_type=F32)

    if final_norm:
        @pl.when(f == pl.num_programs(1) - 1)
        def _():
            o_ref[...] = _rms(o_ref[...], gf_ref[...])


def _ffn(x, g, w1, w3, w2, *, lead=(), in_row0=0, nrows=None, out_rows=None, out_row0=0, prev=None,
         final_g=None, tm=1024, tf=512):
    d = x.shape[1]
    mx = x.shape[0] if nrows is None else nrows
    ff = w1.shape[-1]
    tm = min(tm, mx)
    tf = min(tf, ff)
    out_rows = mx if out_rows is None else out_rows
    rb0 = out_row0 // tm
    ib0 = in_row0 // tm
    in_specs = [
        pl.BlockSpec((tm, d), lambda i, f: (ib0 + i, 0)),
        pl.BlockSpec((1, d), lambda i, f: (0, 0)),
        _stacked_spec(lead, (d, tf), lambda i, f: (0, f)),
        _stacked_spec(lead, (d, tf), lambda i, f: (0, f)),
        _stacked_spec(lead, (tf, d), lambda i, f: (f, 0)),
    ]
    args = [x, g.reshape(1, d), w1, w3, w2]
    if final_g is not None:
        in_specs.append(pl.BlockSpec((1, d), lambda i, f: (0, 0)))
        args.append(final_g.reshape(1, d))
    aliases = _alias_prev(in_specs, args, prev)
    return pl.pallas_call(
        functools.partial(_ffn_kernel, final_norm=final_g is not None),
        grid=(mx // tm, ff // tf),
        in_specs=in_specs,
        out_specs=pl.BlockSpec((tm, d), lambda i, f: (rb0 + i, 0)),
        out_shape=jax.ShapeDtypeStruct((out_rows, d), F32),
        input_output_aliases=aliases,
        scratch_shapes=[pltpu.VMEM((tm, d), BF16)],
        compiler_params=_params("parallel", "arbitrary"),
        name="ffn",
    )(*args)


def _norm_matmul_kernel(x_ref, g_ref, w_ref, o_ref, xn_ref):
    @pl.when(pl.program_id(1) == 0)
    def _():
        xn_ref[...] = _rms(x_ref[...], g_ref[...]).astype(BF16)

    o_ref[...] = jnp.dot(xn_ref[...], w_ref[...], preferred_element_type=F32).astype(o_ref.dtype)


def _norm_matmul(x, g, w, *, lead=(), out_dtype=F32, tm=1024, tn=512):
    m, d = x.shape
    n = w.shape[-1]
    tm = min(tm, m)
    tn = min(tn, n)
    return pl.pallas_call(
        _norm_matmul_kernel,
        grid=(m // tm, n // tn),
        in_specs=[
            pl.BlockSpec((tm, d), lambda i, j: (i, 0)),
            pl.BlockSpec((1, d), lambda i, j: (0, 0)),
            _stacked_spec(lead, (d, tn), lambda i, j: (0, j)),
        ],
        out_specs=pl.BlockSpec((tm, tn), lambda i, j: (i, j)),
        out_shape=jax.ShapeDtypeStruct((m, n), out_dtype),
        scratch_shapes=[pltpu.VMEM((tm, d), BF16)],
        compiler_params=_params("parallel", "arbitrary"),
        name="norm_matmul",
    )(x, g.reshape(1, d), w)


def _rope_tables(lmax, head_dim):
    half = head_dim // 2
    inv = ROPE_THETA ** (-np.arange(0, head_dim, 2, dtype=np.float64) / head_dim)
    ang = np.arange(lmax, dtype=np.float64)[:, None] * inv[None, :]
    ang = np.concatenate([ang, ang], axis=-1)
    cos = np.tile(np.cos(ang), (1, LANES // head_dim))
    sin = np.tile(np.sin(ang), (1, LANES // head_dim))
    first = (np.arange(LANES) % head_dim) < half
    sin_up = np.where(first[None, :], -sin, 0.0)
    sin_dn = np.where(first[None, :], 0.0, sin)
    q_scale = head_dim ** -0.5 * LOG2E
    one, zero = np.ones_like(cos), np.zeros_like(cos)
    stack = lambda q, k, v: jnp.asarray(np.stack([q, k, v]), F32)
    return (stack(cos * q_scale, cos, one), stack(sin_up * q_scale, sin_up, zero),
            stack(sin_dn * q_scale, sin_dn, zero))


def _qkv_kernel(x_ref, g_ref, w_ref, cos_ref, sup_ref, sdn_ref, o_ref, xn_ref, *, half):
    @pl.when(pl.program_id(1) == 0)
    def _():
        xn_ref[...] = _rms(x_ref[...], g_ref[...]).astype(BF16)

    res = jnp.dot(xn_ref[...], w_ref[...], preferred_element_type=F32)
    cos, sup, sdn = cos_ref[...], sup_ref[...], sdn_ref[...]
    for c in range(res.shape[1] // LANES):
        blk = res[:, c * LANES:(c + 1) * LANES]
        rot = blk * cos + pltpu.roll(blk, LANES - half, 1) * sup + pltpu.roll(blk, half, 1) * sdn
        o_ref[:, c * LANES:(c + 1) * LANES] = rot.astype(o_ref.dtype)


def _qkv(x, g, w, tables, groups, *, head_dim, lead=(), tm=1024, tn=512):
    m, d = x.shape
    n = w.shape[-1]
    tm = min(tm, min(l for _, _, l in groups))
    n_sec = d // tn
    cos, sup, sdn = tables

    def table_block(i, j):
        blk = 0
        for row0, nseq, l in groups:
            inside = jnp.logical_and(i * tm >= row0, i * tm < row0 + nseq * l)
            blk = jnp.where(inside, ((i * tm - row0) % l) // tm, blk)
        return (j // n_sec, blk, 0)

    table_spec = pl.BlockSpec((None, tm, LANES), table_block)
    return pl.pallas_call(
        functools.partial(_qkv_kernel, half=head_dim // 2),
        grid=(m // tm, n // tn),
        in_specs=[
            pl.BlockSpec((tm, d), lambda i, j: (i, 0)),
            pl.BlockSpec((1, d), lambda i, j: (0, 0)),
            _stacked_spec(lead, (d, tn), lambda i, j: (0, j)),
            table_spec, table_spec, table_spec,
        ],
        out_specs=pl.BlockSpec((tm, tn), lambda i, j: (i, j)),
        out_shape=jax.ShapeDtypeStruct((m, n), BF16),
        scratch_shapes=[pltpu.VMEM((tm, d), BF16)],
        compiler_params=_params("parallel", "arbitrary"),
        name="qkv_rope",
    )(x, g.reshape(1, d), w, cos, sup, sdn)


def _attn_kernel(lam_ref, g_ref, q_ref, k_ref, v_ref, *rest, tk, tq, lambda_init):
    o_ref, vt_ref, acc_ref, sa_ref, sb_ref = rest[-5:]
    seq_len = k_ref.shape[0]
    nchunk = seq_len // tk
    nsub = q_ref.shape[0] // tq
    half = q_ref.shape[1] // 2

    @pl.when(pl.program_id(2) == 0)
    def _():
        for c in range(nchunk):
            vt_ref[c] = v_ref[c * tk:(c + 1) * tk, :].astype(F32).T.astype(BF16)

    lane = lax.broadcasted_iota(jnp.int32, (tq, q_ref.shape[1]), 1)
    lv = lam_ref[...]
    lam = (jnp.exp(jnp.sum(lv[0:1] * lv[1:2], axis=1, keepdims=True))
           - jnp.exp(jnp.sum(lv[2:3] * lv[3:4], axis=1, keepdims=True)) + lambda_init)

    def rows(j):
        return pl.ds(j * tq if isinstance(j, int) else pl.multiple_of(j * tq, tq), tq)

    def q_parts(j):
        q = q_ref[rows(j), :]
        zero = jnp.zeros_like(q)
        return jnp.where(lane < half, q, zero), jnp.where(lane >= half, q, zero)

    def score_step(c, ms, qs, s_out):
        k = k_ref[pl.ds(pl.multiple_of(c * tk, tk), tk), :]
        new = []
        for comp in range(2):
            s = lax.dot_general(k, qs[comp], (((1,), (1,)), ((), ())), preferred_element_type=F32)
            s_out[comp, c] = s
            new.append(jnp.maximum(ms[comp], jnp.max(s, axis=0, keepdims=True)))
        return tuple(new)

    def pv_step(c, ls, ms, s_in):
        vt = vt_ref[c]
        new = []
        for comp in range(2):
            p = jnp.exp2(s_in[comp, c] - ms[comp])
            new.append(ls[comp] + jnp.sum(p, axis=0, keepdims=True))
            acc_ref[comp] += jnp.dot(vt, p.astype(BF16), preferred_element_type=F32)
        return tuple(new)

    def finalize(j, ls):
        ot = acc_ref[0] * (1.0 / ls[0]) - acc_ref[1] * (lam * (1.0 / ls[1]))
        ms2 = jnp.mean(ot * ot, axis=0, keepdims=True)
        ot = ot * lax.rsqrt(ms2 + NORM_EPS) * g_ref[...] * (1.0 - lambda_init)
        o_ref[rows(j), :] = ot.T.astype(o_ref.dtype)

    neg = (jnp.full((1, tq), -jnp.inf, F32),) * 2
    zero_rows = (jnp.zeros((1, tq), F32),) * 2

    qs0 = q_parts(0)
    ms0 = lax.fori_loop(0, nchunk, lambda c, m: score_step(c, m, qs0, sa_ref), neg, unroll=min(4, nchunk))

    def fused(j, ms, s_in, s_out):
        acc_ref[...] = jnp.zeros_like(acc_ref)
        qn = q_parts(j + 1)

        def both(c, carry):
            ls, mn = carry
            return pv_step(c, ls, ms, s_in), score_step(c, mn, qn, s_out)

        ls, ms_next = lax.fori_loop(0, nchunk, both, (zero_rows, neg), unroll=min(8, nchunk))
        finalize(j, ls)
        return ms_next

    def pair(jj, ms):
        return fused(2 * jj + 1, fused(2 * jj, ms, sa_ref, sb_ref), sb_ref, sa_ref)

    ms_last = lax.fori_loop(0, (nsub - 1) // 2, pair, ms0)
    if (nsub - 1) % 2:
        ms_last = fused(nsub - 2, ms_last, sa_ref, sb_ref)
    last = nsub - 1
    s_last = sb_ref if last % 2 else sa_ref
    acc_ref[...] = jnp.zeros_like(acc_ref)
    ls = lax.fori_loop(0, nchunk, lambda c, l: pv_step(c, l, ms_last, s_last), zero_rows, unroll=min(4, nchunk))
    finalize(last, ls)


def _diff_attention_group(qkv, lam_vec, subln_g, row0, nseq, seq_len, lambda_init, prev, *, tq=256, nsub=16,
                          tk=512):
    m = qkv.shape[0]
    d = qkv.shape[1] // 3
    hw = d // ATT_HEADS
    tq = min(tq, seq_len)
    tk = min(tk, seq_len)
    bq = min(nsub * tq, seq_len)
    nq = seq_len // bq
    qb0 = row0 // bq
    kb0 = row0 // seq_len
    in_specs = [
        pl.BlockSpec(lam_vec.shape, lambda b, h, i: (0, 0)),
        pl.BlockSpec((hw, 1), lambda b, h, i: (0, 0)),
        pl.BlockSpec((bq, hw), lambda b, h, i: (qb0 + b * nq + i, h)),
        pl.BlockSpec((seq_len, hw), lambda b, h, i: (kb0 + b, ATT_HEADS + h)),
        pl.BlockSpec((seq_len, hw), lambda b, h, i: (kb0 + b, 2 * ATT_HEADS + h)),
    ]
    args = [lam_vec, subln_g.reshape(hw, 1), qkv, qkv, qkv]
    aliases = _alias_prev(in_specs, args, prev)
    return pl.pallas_call(
        functools.partial(_attn_kernel, tk=tk, tq=tq, lambda_init=lambda_init),
        grid=(nseq, ATT_HEADS, nq),
        in_specs=in_specs,
        out_specs=pl.BlockSpec((bq, hw), lambda b, h, i: (qb0 + b * nq + i, h)),
        out_shape=jax.ShapeDtypeStruct((m, d), BF16),
        input_output_aliases=aliases,
        scratch_shapes=[
            pltpu.VMEM((seq_len // tk, hw, tk), BF16),
            pltpu.VMEM((2, hw, tq), F32),
            pltpu.VMEM((2, seq_len // tk, tk, tq), F32),
            pltpu.VMEM((2, seq_len // tk, tk, tq), F32),
        ],
        compiler_params=_params("parallel", "parallel", "arbitrary"),
        name="diff_attention",
    )(*args)


def _proj_res_kernel(*refs, nparts):
    res_ref = refs[0]
    a_refs = refs[1:1 + nparts]
    w_refs = refs[1 + nparts:1 + 2 * nparts]
    o_ref = refs[1 + 2 * nparts]
    acc = res_ref[...]
    for a_ref, w_ref in zip(a_refs, w_refs):
        acc = acc + jnp.dot(a_ref[...].astype(BF16), w_ref[...], preferred_element_type=F32)
    o_ref[...] = acc


def _proj_residual(res, parts, w, *, lead=(), tm=512, tn=2048):
    m, n = res.shape
    tm = min(tm, m)
    tn = min(tn, n)
    nparts = len(parts)
    kp = w.shape[-2] // nparts
    in_specs = [pl.BlockSpec((tm, tn), lambda i, j: (i, j))]
    in_specs += [pl.BlockSpec((tm, kp), lambda i, j: (i, 0)) for _ in parts]
    in_specs += [_stacked_spec(lead, (kp, tn), functools.partial(lambda i, j, p: (p, j), p=p))
                 for p in range(nparts)]
    return pl.pallas_call(
        functools.partial(_proj_res_kernel, nparts=nparts),
        grid=(m // tm, n // tn),
        in_specs=in_specs,
        out_specs=pl.BlockSpec((tm, tn), lambda i, j: (i, j)),
        out_shape=jax.ShapeDtypeStruct((m, n), F32),
        compiler_params=_params("parallel", "arbitrary"),
        name="proj_residual",
    )(res, *parts, *([w] * nparts))


def _is_any(i, values):
    hit = i == values[0]
    for v in values[1:]:
        hit = jnp.logical_or(hit, i == v)
    return hit


def _hyena_pre_kernel(*refs, tt, first_tiles, last_tiles):
    (x0m, x0p, x0n, x1m, x1p, x1n, vm, vp, vn, w0, w1, w2, b0, b1, b2, x0_out, u_out) = refs
    i = pl.program_id(0)
    first = _is_any(i, first_tiles)
    last = _is_any(i, last_tiles)

    def conv(m_ref, p_ref, n_ref, w_ref, b_ref):
        x = m_ref[...]
        row = lax.broadcasted_iota(jnp.int32, x.shape, 0)
        prev_row = jnp.where(first, 0.0, p_ref[SUBLANES - 1:SUBLANES, :])
        next_row = jnp.where(last, 0.0, n_ref[0:1, :])
        xm = jnp.where(row == 0, prev_row, pltpu.roll(x, 1, 0))
        xp = jnp.where(row == tt - 1, next_row, pltpu.roll(x, tt - 1, 0))
        return b_ref[...] + xm * w_ref[0:1, :] + x * w_ref[1:2, :] + xp * w_ref[2:3, :]

    x0_out[...] = conv(x0m, x0p, x0n, w0, b0)
    u_out[...] = conv(vm, vp, vn, w2, b2) * conv(x1m, x1p, x1n, w1, b1)


def _hyena_pre(p, conv_w, conv_b, groups, width, *, tt=512, ct=512):
    m = p.shape[0]
    tt = min(tt, min(l for _, _, l in groups))
    ct = min(ct, width)
    nc = width // ct
    first_tiles = tuple((row0 + s * l) // tt for row0, nseq, l in groups for s in range(nseq))
    last_tiles = tuple((row0 + (s + 1) * l) // tt - 1 for row0, nseq, l in groups for s in range(nseq))
    hb = tt // SUBLANES
    nhb = m // SUBLANES

    def main(sec):
        return pl.BlockSpec((tt, ct), lambda i, j: (i, sec * nc + j))

    def prev(sec):
        return pl.BlockSpec((SUBLANES, ct), lambda i, j: (jnp.maximum(i * hb - 1, 0), sec * nc + j))

    def nxt(sec):
        return pl.BlockSpec((SUBLANES, ct), lambda i, j: (jnp.minimum((i + 1) * hb, nhb - 1), sec * nc + j))

    def wspec(sec):
        return pl.BlockSpec((HYENA_SHORT, ct), lambda i, j: (0, sec * nc + j))

    def bspec(sec):
        return pl.BlockSpec((1, ct), lambda i, j: (0, sec * nc + j))

    in_specs = []
    for sec in range(3):
        in_specs += [main(sec), prev(sec), nxt(sec)]
    in_specs += [wspec(0), wspec(1), wspec(2), bspec(0), bspec(1), bspec(2)]
    out_spec = pl.BlockSpec((tt, ct), lambda i, j: (i, j))
    cb = conv_b.reshape(1, -1)
    return pl.pallas_call(
        functools.partial(_hyena_pre_kernel, tt=tt, first_tiles=first_tiles, last_tiles=last_tiles),
        grid=(m // tt, nc),
        in_specs=in_specs,
        out_specs=[out_spec, out_spec],
        out_shape=[jax.ShapeDtypeStruct((m, width), F32)] * 2,
        compiler_params=_params("parallel", "arbitrary"),
        name="hyena_short_conv",
    )(*([p] * 9), conv_w, conv_w, conv_w, cb, cb, cb)


def _hyena_positions(seq_len, tr):
    pos = np.arange(seq_len, dtype=np.float64)[:, None]
    t = pos / max(seq_len - 1, 1)
    w = (2.0 * math.pi / seq_len) * pos
    bands = np.linspace(1e-4, HYENA_BANDS - 1, HYENA_BANDS, dtype=np.float64)[None, :]
    z = np.concatenate([t, np.cos(bands * w), -np.sin(bands * w)], axis=-1)
    z2 = np.concatenate([z, z[:1], z[:0:-1]], axis=0)
    out = np.zeros((2 * seq_len, 64), np.float32)
    out[:, :HYENA_EMB] = z2
    out = out.reshape(2 * seq_len // tr, 2, tr // 2, 64).transpose(0, 2, 1, 3)
    return jnp.asarray(out.reshape(seq_len, 128))


def _filter_kernel(z_ref, win_ref, b_ref, whid_ref, fr_ref, wout_ref, dl_ref, o_ref, *, seq_len):
    hr, packed = z_ref.shape
    half = packed // 2
    hp = lax.Precision.HIGHEST
    z = z_ref[...]
    fr = fr_ref[...]
    h = jnp.sin(fr * (jnp.dot(z, win_ref[...], precision=hp, preferred_element_type=F32) + b_ref[0:1, :]))
    for j in range(HYENA_INNER):
        h = jnp.sin(fr * (jnp.dot(h, whid_ref[j], precision=hp, preferred_element_type=F32)
                          + b_ref[j + 1:j + 2, :]))
    lane = lax.broadcasted_iota(jnp.int32, h.shape, 1)
    row = 2 * hr * pl.program_id(0) + lax.broadcasted_iota(jnp.int32, (hr, o_ref.shape[1]), 0)
    for part in range(2):
        hm = jnp.where((lane >= half) == bool(part), h, 0.0)
        k = jnp.dot(hm, wout_ref[...], precision=hp, preferred_element_type=F32)
        k = k * jnp.exp(-z[:, part * half:part * half + 1] * dl_ref[...])
        o_ref[part * hr:(part + 1) * hr, :] = jnp.where(row + part * hr == seq_len, 0.0, k)


def _hyena_filter(seq_len, w_in, b, w_hid, freq, w_out, *, tr=512):
    order = w_in.shape[1]
    width = w_out.shape[1] // 2
    tr = min(tr, seq_len)
    nfwd = seq_len // tr
    z2 = _hyena_positions(seq_len, tr)
    eye2 = jnp.eye(2, dtype=F32)
    w_in_p = jnp.kron(eye2, jnp.zeros((64, order), F32).at[:HYENA_EMB].set(w_in))
    w_hid_p = jnp.stack([jnp.kron(eye2, w_hid[j]) for j in range(HYENA_INNER)])
    w_out_p = jnp.concatenate([w_out, w_out], axis=0)
    deltas = np.abs(np.linspace(math.log(HYENA_TARGET) / HYENA_SLOW, math.log(HYENA_TARGET) / HYENA_FAST,
                                width, dtype=np.float64))[None, :].astype(np.float32)
    return pl.pallas_call(
        functools.partial(_filter_kernel, seq_len=seq_len),
        grid=(2 * nfwd,),
        in_specs=[
            pl.BlockSpec((tr // 2, z2.shape[1]), lambda i: (i, 0)),
            pl.BlockSpec(w_in_p.shape, lambda i: (0, 0)),
            pl.BlockSpec((b.shape[0], 2 * order), lambda i: (0, 0)),
            pl.BlockSpec(w_hid_p.shape, lambda i: (0, 0, 0)),
            pl.BlockSpec((1, 2 * order), lambda i: (0, 0)),
            pl.BlockSpec((2 * order, width), lambda i: (0, i // nfwd)),
            pl.BlockSpec((1, width), lambda i: (0, 0)),
        ],
        out_specs=pl.BlockSpec((tr, width), lambda i: (i, 0)),
        out_shape=jax.ShapeDtypeStruct((2 * seq_len, width), F32),
        compiler_params=_params("parallel"),
        name="hyena_filter",
    )(z2, w_in_p, jnp.tile(b, (1, 2)), w_hid_p, jnp.tile(freq.reshape(1, order), (1, 2)), w_out_p,
      jnp.asarray(deltas))


def _dft_constants(seq_len):
    n = 2 * seq_len
    n2 = FFT_INNER
    n1 = n // n2
    a1 = 2.0 * np.pi * np.outer(np.arange(n1), np.arange(n1)) / n1
    f1 = np.concatenate([np.cos(a1), -np.sin(a1)], axis=0)
    f3 = np.concatenate([np.cos(a1), -np.sin(a1)], axis=1)[:n1 // 2]
    a2 = 2.0 * np.pi * np.outer(np.arange(n2), np.arange(n2)) / n2
    gr, gi = np.cos(a2), -np.sin(a2)
    g_fwd = np.block([[gr, -gi], [gi, gr]])
    g_inv = np.block([[gr, gi], [-gi, gr]])
    at = 2.0 * np.pi * np.outer(np.arange(n1), np.arange(n2)) / n
    tw_r = np.cos(at)[:, :, None]
    tw_i = -np.sin(at)[:, :, None]
    c = lambda a, dt: jnp.asarray(a, dt)
    return dict(n1=n1, n2=n2, f1_full=c(f1, BF16), f1_half=c(f1[:, :n1 // 2], BF16), f3=c(f3, BF16),
                g_fwd=c(g_fwd, BF16), g_inv=c(g_inv, BF16), tw_r=c(tw_r, F32), tw_i=c(tw_i, F32))


def _dft_outer_fwd_kernel(f_ref, x_ref, o_ref):
    f = f_ref[...]
    for s in range(x_ref.shape[1]):
        o_ref[:, s, :] = jnp.dot(f, x_ref[:, s, :].astype(BF16), preferred_element_type=F32)


def _dft_outer_forward(fmat, x, seq0, nseq, *, ct=1024):
    _, k, n2, c = x.shape
    mo = fmat.shape[0]
    ct = min(ct, c)
    return pl.pallas_call(
        _dft_outer_fwd_kernel,
        grid=(nseq, n2 // SUBLANES, c // ct),
        in_specs=[
            pl.BlockSpec((mo, k), lambda b, s, j: (0, 0)),
            pl.BlockSpec((None, k, SUBLANES, ct), lambda b, s, j: (seq0 + b, 0, s, j)),
        ],
        out_specs=pl.BlockSpec((None, mo, SUBLANES, ct), lambda b, s, j: (b, 0, s, j)),
        out_shape=jax.ShapeDtypeStruct((nseq, mo, n2, c), F32),
        compiler_params=_params("parallel", "parallel", "parallel"),
        name="dft_outer_forward",
    )(fmat, x)


def _twiddle_dft(ar_ref, ai_ref, twr_ref, twi_ref, g_ref):
    n2 = ar_ref.shape[0]
    ar, ai = ar_ref[...], ai_ref[...]
    tr, ti = twr_ref[...], twi_ref[...]
    z = jnp.concatenate([ar * tr - ai * ti, ar * ti + ai * tr], axis=0).astype(BF16)
    x = jnp.dot(g_ref[...], z, preferred_element_type=F32)
    return x[:n2], x[n2:]


def _spectrum_kernel(ar_ref, ai_ref, twr_ref, twi_ref, g_ref, kr_ref, ki_ref, *, scale):
    xr, xi = _twiddle_dft(ar_ref, ai_ref, twr_ref, twi_ref, g_ref)
    kr_ref[...] = xr * scale
    ki_ref[...] = xi * scale


def _fft_mid_kernel(ar_ref, ai_ref, twr_ref, twi_ref, g_ref, gi_ref, kr_ref, ki_ref, br_ref, bi_ref):
    n2 = ar_ref.shape[0]
    xr, xi = _twiddle_dft(ar_ref, ai_ref, twr_ref, twi_ref, g_ref)
    kr, ki = kr_ref[...], ki_ref[...]
    y = jnp.concatenate([xr * kr - xi * ki, xr * ki + xi * kr], axis=0).astype(BF16)
    b = jnp.dot(gi_ref[...], y, preferred_element_type=F32)
    br, bi = b[:n2], b[n2:]
    tr, ti = twr_ref[...], twi_ref[...]
    br_ref[...] = br * tr + bi * ti
    bi_ref[...] = bi * tr - br * ti


def _fft_mid(a, consts, kf=None, *, ct=1024):
    nb, _, n2, c = a.shape
    n1 = consts["n1"]
    ct = min(ct, c)
    grid = (nb, n1, c // ct)
    blk = lambda im: pl.BlockSpec((None, None, n2, ct), lambda b, k, j: (b, im * n1 + k, 0, j))
    tw = pl.BlockSpec((None, n2, 1), lambda b, k, j: (k, 0, 0))
    gs = pl.BlockSpec((2 * n2, 2 * n2), lambda b, k, j: (0, 0))
    half_spec = pl.BlockSpec((None, None, n2, ct), lambda b, k, j: (b, k, 0, j))
    half_shape = jax.ShapeDtypeStruct((nb, n1, n2, c), F32)
    if kf is None:
        return pl.pallas_call(
            functools.partial(_spectrum_kernel, scale=1.0 / (n1 * n2)),
            grid=grid,
            in_specs=[blk(0), blk(1), tw, tw, gs],
            out_specs=[half_spec, half_spec],
            out_shape=[half_shape, half_shape],
            compiler_params=_params("parallel", "parallel", "parallel"),
            name="dft_filter_spectrum",
        )(a, a, consts["tw_r"], consts["tw_i"], consts["g_fwd"])
    kblk = lambda: pl.BlockSpec((None, None, n2, ct), lambda b, k, j: (0, k, 0, j))
    return pl.pallas_call(
        _fft_mid_kernel,
        grid=grid,
        in_specs=[blk(0), blk(1), tw, tw, gs, gs, kblk(), kblk()],
        out_specs=[half_spec, half_spec],
        out_shape=[half_shape, half_shape],
        compiler_params=_params("parallel", "parallel", "parallel"),
        name="dft_inner_filter",
    )(a, a, consts["tw_r"], consts["tw_i"], consts["g_fwd"], consts["g_inv"], kf[0], kf[1])


def _dft_outer_inv_kernel(f_ref, br_ref, bi_ref, u_ref, x0_ref, bias_ref, *rest):
    o_ref = rest[-1]
    f = f_ref[...]
    bias = bias_ref[...]
    for s in range(u_ref.shape[1]):
        b = jnp.concatenate([br_ref[:, s, :], bi_ref[:, s, :]], axis=0).astype(BF16)
        y = jnp.dot(f, b, preferred_element_type=F32)
        o_ref[:, s, :] = x0_ref[:, s, :] * (y + u_ref[:, s, :] * bias)


def _dft_outer_inverse(f3, br, bi, u, x0, bias, seq0, prev, *, ct=1024):
    nseq, n1, n2, c = br.shape
    mo = f3.shape[0]
    ct = min(ct, c)
    spec_in = pl.BlockSpec((None, n1, SUBLANES, ct), lambda b, s, j: (b, 0, s, j))
    spec_tok = pl.BlockSpec((None, mo, SUBLANES, ct), lambda b, s, j: (seq0 + b, 0, s, j))
    in_specs = [pl.BlockSpec(f3.shape, lambda b, s, j: (0, 0)), spec_in, spec_in, spec_tok, spec_tok,
                pl.BlockSpec((1, ct), lambda b, s, j: (0, j))]
    args = [f3, br, bi, u, x0, bias.reshape(1, c)]
    aliases = _alias_prev(in_specs, args, None if prev is None else prev.reshape(u.shape))
    return pl.pallas_call(
        _dft_outer_inv_kernel,
        grid=(nseq, n2 // SUBLANES, c // ct),
        in_specs=in_specs,
        out_specs=spec_tok,
        out_shape=jax.ShapeDtypeStruct(u.shape, F32),
        input_output_aliases=aliases,
        compiler_params=_params("parallel", "parallel", "parallel"),
        name="dft_outer_inverse",
    )(*args)


def _hyena_long_conv_group(u, x0, k_two, bias, row0, nseq, seq_len, prev):
    m, c = u.shape
    consts = _dft_constants(seq_len)
    n1, n2 = consts["n1"], consts["n2"]
    ak = _dft_outer_forward(consts["f1_full"], k_two.reshape(1, n1, n2, c), 0, 1)
    kfr, kfi = _fft_mid(ak, consts)
    tok_view = (m // seq_len, n1 // 2, n2, c)
    seq0 = row0 // seq_len
    a = _dft_outer_forward(consts["f1_half"], u.reshape(tok_view), seq0, nseq)
    br, bi = _fft_mid(a, consts, kf=(kfr, kfi))
    y = _dft_outer_inverse(consts["f3"], br, bi, u.reshape(tok_view), x0.reshape(tok_view), bias, seq0, prev)
    return y.reshape(m, c)


def _softplus(z):
    return jnp.maximum(z, 0.0) + jnp.log(1.0 + jnp.exp(-jnp.abs(z)))


def _lru_kernel(pl_ref, pg_ref, cw_ref, cb_ref, wa_ref, ba_ref, wx_ref, bx_ref, lam_ref, *rest, tchunk):
    o_ref, xp_ref, hs_ref = rest[-3:]
    seq_len, cw = pl_ref.shape
    nchunk = seq_len // tchunk
    halo = SUBLANES
    xp_ref[0:halo, :] = jnp.zeros((halo, cw), F32)
    xp_ref[halo + seq_len:2 * halo + seq_len, :] = jnp.zeros((halo, cw), F32)
    xp_ref[halo:halo + seq_len, :] = pl_ref[...]

    row = lax.broadcasted_iota(jnp.int32, (tchunk, cw), 0)
    nsteps = int(math.log2(tchunk))
    neg_c_softplus = -LRU_C * _softplus(-lam_ref[...])

    def gates(c, direction):
        base = pl.multiple_of(c * tchunk, tchunk)
        xw = xp_ref[pl.ds(base, tchunk + 2 * halo), :]
        ext = tchunk + 2 * halo
        xb = cb_ref[...]
        for j in range(LRU_CONV):
            shift = (LRU_CONV // 2 - j) % ext
            tap = xw if shift == 0 else pltpu.roll(xw, shift, 0)
            xb = xb + tap[halo:halo + tchunk, :] * cw_ref[j:j + 1, :]
        xb16 = xb.astype(BF16)
        r = jax.nn.sigmoid(jnp.dot(xb16, wa_ref[direction], preferred_element_type=F32)
                           + ba_ref[direction:direction + 1, :])
        i = jax.nn.sigmoid(jnp.dot(xb16, wx_ref[direction], preferred_element_type=F32)
                           + bx_ref[direction:direction + 1, :])
        log_a = r * neg_c_softplus[direction:direction + 1, :]
        a = jnp.exp(log_a)
        gain = jnp.sqrt(-jnp.tanh(log_a) * (a * a + 1.0))
        return a, gain * (i * xb)

    def scan(a, b, reverse):
        for s in range(nsteps):
            d = 1 << s
            if reverse:
                keep = row < tchunk - d
                a_sh = pltpu.roll(a, tchunk - d, 0)
                b_sh = pltpu.roll(b, tchunk - d, 0)
            else:
                keep = row >= d
                a_sh = pltpu.roll(a, d, 0)
                b_sh = pltpu.roll(b, d, 0)
            b = jnp.where(keep, a * b_sh + b, b)
            a = jnp.where(keep, a * a_sh, a)
        return a, b

    def fwd_body(c, h):
        a, b = gates(c, 0)
        a, b = scan(a, b, False)
        hc = a * h + b
        hs_ref[pl.ds(pl.multiple_of(c * tchunk, tchunk), tchunk), :] = hc
        return hc[tchunk - 1:tchunk, :]

    lax.fori_loop(0, nchunk, fwd_body, jnp.zeros((1, cw), F32), unroll=min(4, nchunk))

    def rev_body(cc, h):
        c = nchunk - 1 - cc
        a, b = gates(c, 1)
        a, b = scan(a, b, True)
        hc = a * h + b
        sl = pl.ds(pl.multiple_of(c * tchunk, tchunk), tchunk)
        o_ref[sl, :] = (hs_ref[sl, :] + hc) * jax.nn.gelu(pg_ref[sl, :])
        return hc[0:1, :]

    lax.fori_loop(0, nchunk, rev_body, jnp.zeros((1, cw), F32), unroll=min(4, nchunk))


def _rg_lru_group(p, col_gate, col_x, conv_w, conv_b, wa, ba, wx, bx, lam, row0, nseq, seq_len, prev, *,
                  lead=(), tchunk=128):
    m = p.shape[0]
    nblk = wa.shape[-3]
    cw = wa.shape[-1]
    tchunk = min(tchunk, seq_len)
    sb0 = row0 // seq_len
    gb0 = col_gate // cw
    xb0 = col_x // cw
    vec = lambda rows: pl.BlockSpec((rows, cw), lambda b, n: (0, n))
    mat = _stacked_spec(lead, (2, None, cw, cw), lambda b, n: (0, n, 0, 0))
    in_specs = [
        pl.BlockSpec((seq_len, cw), lambda b, n: (sb0 + b, xb0 + n)),
        pl.BlockSpec((seq_len, cw), lambda b, n: (sb0 + b, gb0 + n)),
        vec(LRU_CONV), vec(1), mat, vec(2), mat, vec(2), vec(2),
    ]
    args = [p, p, conv_w, conv_b.reshape(1, -1), wa, ba, wx, bx, lam]
    aliases = _alias_prev(in_specs, args, prev)
    return pl.pallas_call(
        functools.partial(_lru_kernel, tchunk=tchunk),
        grid=(nseq, nblk),
        in_specs=in_specs,
        out_specs=pl.BlockSpec((seq_len, cw), lambda b, n: (sb0 + b, n)),
        out_shape=jax.ShapeDtypeStruct((m, nblk * cw), F32),
        input_output_aliases=aliases,
        scratch_shapes=[pltpu.VMEM((seq_len + 2 * SUBLANES, cw), F32), pltpu.VMEM((seq_len, cw), F32)],
        compiler_params=_params("parallel", "parallel"),
        name="rg_lru",
    )(*args)


def kernel(x_prompt, x_sample, ffn_norm, ffn_w1, ffn_w3, ffn_w2, mix_norm, final_norm, even_w_in, hyena_conv_w, hyena_conv_b, hyena_filt_w_in, hyena_filt_b, hyena_filt_w_hid, hyena_filt_freq, hyena_filt_w_out, hyena_bias, lru_conv_w, lru_conv_b, lru_wa, lru_ba, lru_wx, lru_bx, lru_lambda, even_w_out, attn_w_qkv, attn_lambda, attn_subln, attn_w_o):
    d = x_prompt.shape[-1]
    depth = ffn_norm.shape[0]
    groups = []
    row0 = 0
    for xg in (x_prompt, x_sample):
        groups.append((row0, xg.shape[0], xg.shape[1]))
        row0 += xg.shape[0] * xg.shape[1]
    groups = tuple(groups)
    m = row0
    inputs = (x_prompt, x_sample)

    hw = hyena_bias.shape[1]
    head_dim = d // (2 * ATT_HEADS)
    rope = _rope_tables(max(l for _, _, l in groups), head_dim)
    w1, w3, w2 = ffn_w1.astype(BF16), ffn_w3.astype(BF16), ffn_w2.astype(BF16)
    w_in, w_out = even_w_in.astype(BF16), even_w_out.astype(BF16)
    w_qkv, w_o = attn_w_qkv.astype(BF16), attn_w_o.astype(BF16)
    wa, wx = lru_wa.astype(BF16), lru_wx.astype(BF16)
    lru_cols = (3 * hw, 3 * hw + lru_wa.shape[-3] * lru_wa.shape[-1])

    x = None
    for i in range(depth):
        j = i // 2
        if i == 0:
            for (row0, _, _), xg in zip(groups, inputs):
                x = _ffn(xg.reshape(-1, d), ffn_norm[i, 0], w1, w3, w2, lead=(i, 0), out_rows=m, out_row0=row0,
                         prev=x)
        else:
            x = _ffn(x, ffn_norm[i, 0], w1, w3, w2, lead=(i, 0))
        if i % 2 == 0:
            p = _norm_matmul(x, mix_norm[i], w_in, lead=(j,))
            x0, u = _hyena_pre(p, hyena_conv_w[j], hyena_conv_b[j], groups, hw)
            yh = yl = None
            for row0, nseq, seq_len in groups:
                k_two = _hyena_filter(seq_len, hyena_filt_w_in[j], hyena_filt_b[j], hyena_filt_w_hid[j],
                                      hyena_filt_freq[j], hyena_filt_w_out[j])
                yh = _hyena_long_conv_group(u, x0, k_two, hyena_bias[j], row0, nseq, seq_len, yh)
                yl = _rg_lru_group(p, *lru_cols, lru_conv_w[j], lru_conv_b[j], wa, lru_ba[j], wx, lru_bx[j],
                                   lru_lambda[j], row0, nseq, seq_len, yl, lead=(j,))
            x = _proj_residual(x, [yh, yl], w_out, lead=(j,))
        else:
            lambda_init = 0.8 - 0.6 * math.exp(-0.3 * i)
            qkv = _qkv(x, mix_norm[i], w_qkv, rope, groups, head_dim=head_dim, lead=(j,))
            o = None
            for row0, nseq, seq_len in groups:
                o = _diff_attention_group(qkv, attn_lambda[j], attn_subln[j], row0, nseq, seq_len, lambda_init, o)
            x = _proj_residual(x, [o], w_o, lead=(j,))
        if i + 1 < depth:
            x = _ffn(x, ffn_norm[i, 1], w1, w3, w2, lead=(i, 1))

    outs = []
    for (row0, nseq, seq_len), xg in zip(groups, inputs):
        y = _ffn(x, ffn_norm[depth - 1, 1], w1, w3, w2, lead=(depth - 1, 1), in_row0=row0, nrows=nseq * seq_len,
                 final_g=final_norm)
        outs.append(y.reshape(xg.shape))
    return tuple(outs)
```

```python
import functools
import math

import numpy as np
import jax
import jax.numpy as jnp
from jax import lax
from jax.experimental import pallas as pl
from jax.experimental.pallas import tpu as pltpu

F32 = jnp.float32
BF16 = jnp.bfloat16

NORM_EPS = 1e-6
HYENA_SHORT = 3
HYENA_EMB = 33
HYENA_BANDS = (HYENA_EMB - 1) // 2
HYENA_INNER = 2
HYENA_TARGET = 1e-2
HYENA_FAST = 0.3
HYENA_SLOW = 1.5
LRU_BLOCKS = 8
LRU_CONV = 4
LRU_C = 8.0
ATT_HEADS = 16
ROPE_THETA = 10000.0
LOG2E = 1.4426950408889634

LANES = 128
SUBLANES = 8
VMEM_LIMIT_BYTES = 56 * 1024 * 1024
FFT_INNER = 128


def _params(*sem):
    return pltpu.CompilerParams(dimension_semantics=sem, vmem_limit_bytes=VMEM_LIMIT_BYTES)


def _rms(x, g):
    ms = jnp.mean(x * x, axis=-1, keepdims=True)
    return x * lax.rsqrt(ms + NORM_EPS) * g


def _stacked_spec(lead, block, index):
    lead = tuple(lead)
    return pl.BlockSpec((None,) * len(lead) + tuple(block), lambda *g: lead + tuple(index(*g)))


def _alias_prev(in_specs, args, prev):
    if prev is None:
        return {}
    in_specs.append(pl.BlockSpec(memory_space=pl.ANY))
    args.append(prev)
    return {len(args) - 1: 0}


def _ffn_kernel(x_ref, g_ref, w1_ref, w3_ref, w2_ref, *rest, final_norm):
    gf_ref = rest[0] if final_norm else None
    o_ref, xn_ref = rest[-2:]
    f = pl.program_id(1)

    @pl.when(f == 0)
    def _():
        x = x_ref[...]
        xn_ref[...] = _rms(x, g_ref[...]).astype(BF16)
        o_ref[...] = x

    xn = xn_ref[...]
    a = jnp.dot(xn, w1_ref[...], preferred_element_type=F32)
    b = jnp.dot(xn, w3_ref[...], preferred_element_type=F32)
    h = (0.5 * a * jax.nn.sigmoid(a) * b).astype(BF16)
    o_ref[...] += jnp.dot(h, w2_ref[...], preferred_element_type=F32)

    if final_norm:
        @pl.when(f == pl.num_programs(1) - 1)
        def _():
            o_ref[...] = _rms(o_ref[...], gf_ref[...])


def _ffn(x, g, w1, w3, w2, *, lead=(), in_row0=0, nrows=None, out_rows=None, out_row0=0, prev=None,
         final_g=None, tm=1024, tf=512):
    d = x.shape[1]
    mx = x.shape[0] if nrows is None else nrows
    ff = w1.shape[-1]
    tm = min(tm, mx)
    tf = min(tf, ff)
    out_rows = mx if out_rows is None else out_rows
    rb0 = out_row0 // tm
    ib0 = in_row0 // tm
    in_specs = [
        pl.BlockSpec((tm, d), lambda i, f: (ib0 + i, 0)),
        pl.BlockSpec((1, d), lambda i, f: (0, 0)),
        _stacked_spec(lead, (d, tf), lambda i, f: (0, f)),
        _stacked_spec(lead, (d, tf), lambda i, f: (0, f)),
        _stacked_spec(lead, (tf, d), lambda i, f: (f, 0)),
    ]
    args = [x, g.reshape(1, d), w1, w3, w2]
    if final_g is not None:
        in_specs.append(pl.BlockSpec((1, d), lambda i, f: (0, 0)))
        args.append(final_g.reshape(1, d))
    aliases = _alias_prev(in_specs, args, prev)
    return pl.pallas_call(
        functools.partial(_ffn_kernel, final_norm=final_g is not None),
        grid=(mx // tm, ff // tf),
        in_specs=in_specs,
        out_specs=pl.BlockSpec((tm, d), lambda i, f: (rb0 + i, 0)),
        out_shape=jax.ShapeDtypeStruct((out_rows, d), F32),
        input_output_aliases=aliases,
        scratch_shapes=[pltpu.VMEM((tm, d), BF16)],
        compiler_params=_params("parallel", "arbitrary"),
        name="ffn",
    )(*args)


def _norm_matmul_kernel(x_ref, g_ref, w_ref, o_ref, xn_ref):
    @pl.when(pl.program_id(1) == 0)
    def _():
        xn_ref[...] = _rms(x_ref[...], g_ref[...]).astype(BF16)

    o_ref[...] = jnp.dot(xn_ref[...], w_ref[...], preferred_element_type=F32).astype(o_ref.dtype)


def _norm_matmul(x, g, w, *, lead=(), out_dtype=F32, tm=1024, tn=1024):
    m, d = x.shape
    n = w.shape[-1]
    tm = min(tm, m)
    tn = min(tn, n)
    return pl.pallas_call(
        _norm_matmul_kernel,
        grid=(m // tm, n // tn),
        in_specs=[
            pl.BlockSpec((tm, d), lambda i, j: (i, 0)),
            pl.BlockSpec((1, d), lambda i, j: (0, 0)),
            _stacked_spec(lead, (d, tn), lambda i, j: (0, j)),
        ],
        out_specs=pl.BlockSpec((tm, tn), lambda i, j: (i, j)),
        out_shape=jax.ShapeDtypeStruct((m, n), out_dtype),
        scratch_shapes=[pltpu.VMEM((tm, d), BF16)],
        compiler_params=_params("parallel", "arbitrary"),
        name="norm_matmul",
    )(x, g.reshape(1, d), w)


def _rope_tables(lmax, head_dim):
    half = head_dim // 2
    inv = ROPE_THETA ** (-np.arange(0, head_dim, 2, dtype=np.float64) / head_dim)
    ang = np.arange(lmax, dtype=np.float64)[:, None] * inv[None, :]
    ang = np.tile(ang, (1, LANES // half))
    second = (np.arange(LANES) // (LANES // 2)) == 1
    cos = np.cos(ang)
    sin = np.where(second[None, :], np.sin(ang), -np.sin(ang))
    q_scale = head_dim ** -0.5 * LOG2E
    stack = lambda q, k, v: jnp.asarray(np.stack([q, k, v]), F32)
    return (stack(cos * q_scale, cos, np.ones_like(cos)), stack(sin * q_scale, sin, np.zeros_like(sin)))


def _permute_qk_columns(w, d, head_dim):
    lead = w.shape[:-1]
    qk = w[..., :2 * d].reshape(*lead, 2 * d // (2 * head_dim), 2, 2, head_dim // 2)
    qk = jnp.swapaxes(qk, -3, -2).reshape(*lead, 2 * d)
    return jnp.concatenate([qk, w[..., 2 * d:]], axis=-1)


def _qkv_kernel(x_ref, g_ref, w_ref, cos_ref, sin_ref, o_ref, xn_ref):
    @pl.when(pl.program_id(1) == 0)
    def _():
        xn_ref[...] = _rms(x_ref[...], g_ref[...]).astype(BF16)

    res = jnp.dot(xn_ref[...], w_ref[...], preferred_element_type=F32)
    cos, sin = cos_ref[...], sin_ref[...]
    for c in range(res.shape[1] // LANES):
        blk = res[:, c * LANES:(c + 1) * LANES]
        rot = blk * cos + pltpu.roll(blk, LANES // 2, 1) * sin
        o_ref[:, c * LANES:(c + 1) * LANES] = rot.astype(o_ref.dtype)


def _qkv(x, g, w, tables, groups, *, lead=(), tm=1024, tn=512):
    m, d = x.shape
    n = w.shape[-1]
    tm = min(tm, min(l for _, _, l in groups))
    n_sec = d // tn
    cos, sin = tables

    def table_block(i, j):
        blk = 0
        for row0, nseq, l in groups:
            inside = jnp.logical_and(i * tm >= row0, i * tm < row0 + nseq * l)
            blk = jnp.where(inside, ((i * tm - row0) % l) // tm, blk)
        return (j // n_sec, blk, 0)

    table_spec = pl.BlockSpec((None, tm, LANES), table_block)
    return pl.pallas_call(
        _qkv_kernel,
        grid=(m // tm, n // tn),
        in_specs=[
            pl.BlockSpec((tm, d), lambda i, j: (i, 0)),
            pl.BlockSpec((1, d), lambda i, j: (0, 0)),
            _stacked_spec(lead, (d, tn), lambda i, j: (0, j)),
            table_spec, table_spec,
        ],
        out_specs=pl.BlockSpec((tm, tn), lambda i, j: (i, j)),
        out_shape=jax.ShapeDtypeStruct((m, n), BF16),
        scratch_shapes=[pltpu.VMEM((tm, d), BF16)],
        compiler_params=_params("parallel", "arbitrary"),
        name="qkv_rope",
    )(x, g.reshape(1, d), w, cos, sin)


def _attn_kernel(lam_ref, g_ref, q_ref, k_ref, v_ref, *rest, tk, tq, lambda_init):
    o_ref, vt_ref, acc_ref, sa_ref, sb_ref = rest[-5:]
    seq_len = k_ref.shape[0]
    nchunk = seq_len // tk
    nsub = q_ref.shape[0] // tq

    @pl.when(pl.program_id(2) == 0)
    def _():
        for c in range(nchunk):
            vt_ref[c] = v_ref[c * tk:(c + 1) * tk, :].astype(F32).T.astype(BF16)

    comp1 = (lax.broadcasted_iota(jnp.int32, (tq, q_ref.shape[1]), 1) // (q_ref.shape[1] // 4)) % 2 == 1
    lv = lam_ref[...]
    lam = (jnp.exp(jnp.sum(lv[0:1] * lv[1:2], axis=1, keepdims=True))
           - jnp.exp(jnp.sum(lv[2:3] * lv[3:4], axis=1, keepdims=True)) + lambda_init)

    def rows(j):
        return pl.ds(j * tq if isinstance(j, int) else pl.multiple_of(j * tq, tq), tq)

    def q_parts(j):
        q = q_ref[rows(j), :]
        zero = jnp.zeros_like(q)
        return jnp.where(comp1, zero, q), jnp.where(comp1, q, zero)

    def score_step(c, ms, qs, s_out):
        k = k_ref[pl.ds(pl.multiple_of(c * tk, tk), tk), :]
        new = []
        for comp in range(2):
            s = lax.dot_general(k, qs[comp], (((1,), (1,)), ((), ())), preferred_element_type=F32)
            s_out[comp, c] = s
            new.append(jnp.maximum(ms[comp], jnp.max(s, axis=0, keepdims=True)))
        return tuple(new)

    def pv_step(c, ls, ms, s_in):
        vt = vt_ref[c]
        new = []
        for comp in range(2):
            p = jnp.exp2(s_in[comp, c] - ms[comp])
            new.append(ls[comp] + jnp.sum(p, axis=0, keepdims=True))
            acc_ref[comp] += jnp.dot(vt, p.astype(BF16), preferred_element_type=F32)
        return tuple(new)

    def finalize(j, ls):
        ot = acc_ref[0] * (1.0 / ls[0]) - acc_ref[1] * (lam * (1.0 / ls[1]))
        ms2 = jnp.mean(ot * ot, axis=0, keepdims=True)
        ot = ot * lax.rsqrt(ms2 + NORM_EPS) * g_ref[...] * (1.0 - lambda_init)
        o_ref[rows(j), :] = ot.T.astype(o_ref.dtype)

    neg = (jnp.full((1, tq), -jnp.inf, F32),) * 2
    zero_rows = (jnp.zeros((1, tq), F32),) * 2

    qs0 = q_parts(0)
    ms0 = lax.fori_loop(0, nchunk, lambda c, m: score_step(c, m, qs0, sa_ref), neg, unroll=min(8, nchunk))

    def fused(j, ms, s_in, s_out):
        acc_ref[...] = jnp.zeros_like(acc_ref)
        qn = q_parts(j + 1)

        def both(c, carry):
            ls, mn = carry
            return pv_step(c, ls, ms, s_in), score_step(c, mn, qn, s_out)

        ls, ms_next = lax.fori_loop(0, nchunk, both, (zero_rows, neg), unroll=min(8, nchunk))
        finalize(j, ls)
        return ms_next

    def pair(jj, ms):
        return fused(2 * jj + 1, fused(2 * jj, ms, sa_ref, sb_ref), sb_ref, sa_ref)

    ms_last = lax.fori_loop(0, (nsub - 1) // 2, pair, ms0)
    if (nsub - 1) % 2:
        ms_last = fused(nsub - 2, ms_last, sa_ref, sb_ref)
    last = nsub - 1
    s_last = sb_ref if last % 2 else sa_ref
    acc_ref[...] = jnp.zeros_like(acc_ref)
    ls = lax.fori_loop(0, nchunk, lambda c, l: pv_step(c, l, ms_last, s_last), zero_rows, unroll=min(8, nchunk))
    finalize(last, ls)


def _diff_attention_group(qkv, lam_vec, subln_g, row0, nseq, seq_len, lambda_init, prev, *, tq=256, nsub=16,
                          tk=512):
    m = qkv.shape[0]
    d = qkv.shape[1] // 3
    hw = d // ATT_HEADS
    tq = min(tq, seq_len)
    tk = min(tk, seq_len)
    bq = min(nsub * tq, seq_len)
    nq = seq_len // bq
    qb0 = row0 // bq
    kb0 = row0 // seq_len
    in_specs = [
        pl.BlockSpec(lam_vec.shape, lambda b, h, i: (0, 0)),
        pl.BlockSpec((hw, 1), lambda b, h, i: (0, 0)),
        pl.BlockSpec((bq, hw), lambda b, h, i: (qb0 + b * nq + i, h)),
        pl.BlockSpec((seq_len, hw), lambda b, h, i: (kb0 + b, ATT_HEADS + h)),
        pl.BlockSpec((seq_len, hw), lambda b, h, i: (kb0 + b, 2 * ATT_HEADS + h)),
    ]
    args = [lam_vec, subln_g.reshape(hw, 1), qkv, qkv, qkv]
    aliases = _alias_prev(in_specs, args, prev)
    return pl.pallas_call(
        functools.partial(_attn_kernel, tk=tk, tq=tq, lambda_init=lambda_init),
        grid=(nseq, ATT_HEADS, nq),
        in_specs=in_specs,
        out_specs=pl.BlockSpec((bq, hw), lambda b, h, i: (qb0 + b * nq + i, h)),
        out_shape=jax.ShapeDtypeStruct((m, d), BF16),
        input_output_aliases=aliases,
        scratch_shapes=[
            pltpu.VMEM((seq_len // tk, hw, tk), BF16),
            pltpu.VMEM((2, hw, tq), F32),
            pltpu.VMEM((2, seq_len // tk, tk, tq), F32),
            pltpu.VMEM((2, seq_len // tk, tk, tq), F32),
        ],
        compiler_params=_params("parallel", "parallel", "arbitrary"),
        name="diff_attention",
    )(*args)


def _proj_res_kernel(*refs, nparts):
    res_ref = refs[0]
    a_refs = refs[1:1 + nparts]
    w_refs = refs[1 + nparts:1 + 2 * nparts]
    o_ref = refs[1 + 2 * nparts]
    acc = res_ref[...]
    for a_ref, w_ref in zip(a_refs, w_refs):
        acc = acc + jnp.dot(a_ref[...].astype(BF16), w_ref[...], preferred_element_type=F32)
    o_ref[...] = acc


def _proj_residual(res, parts, w, *, lead=(), tm=512, tn=2048):
    m, n = res.shape
    tm = min(tm, m)
    tn = min(tn, n)
    nparts = len(parts)
    kp = w.shape[-2] // nparts
    in_specs = [pl.BlockSpec((tm, tn), lambda i, j: (i, j))]
    in_specs += [pl.BlockSpec((tm, kp), lambda i, j: (i, 0)) for _ in parts]
    in_specs += [_stacked_spec(lead, (kp, tn), functools.partial(lambda i, j, p: (p, j), p=p))
                 for p in range(nparts)]
    return pl.pallas_call(
        functools.partial(_proj_res_kernel, nparts=nparts),
        grid=(m // tm, n // tn),
        in_specs=in_specs,
        out_specs=pl.BlockSpec((tm, tn), lambda i, j: (i, j)),
        out_shape=jax.ShapeDtypeStruct((m, n), F32),
        compiler_params=_params("parallel", "arbitrary"),
        name="proj_residual",
    )(res, *parts, *([w] * nparts))


def _is_any(i, values):
    hit = i == values[0]
    for v in values[1:]:
        hit = jnp.logical_or(hit, i == v)
    return hit


def _hyena_pre_kernel(*refs, tt, first_tiles, last_tiles):
    (x0m, x0p, x0n, x1m, x1p, x1n, vm, vp, vn, w0, w1, w2, b0, b1, b2, x0_out, u_out) = refs
    i = pl.program_id(0)
    first = _is_any(i, first_tiles)
    last = _is_any(i, last_tiles)

    def conv(m_ref, p_ref, n_ref, w_ref, b_ref):
        x = m_ref[...]
        row = lax.broadcasted_iota(jnp.int32, x.shape, 0)
        prev_row = jnp.where(first, 0.0, p_ref[SUBLANES - 1:SUBLANES, :])
        next_row = jnp.where(last, 0.0, n_ref[0:1, :])
        xm = jnp.where(row == 0, prev_row, pltpu.roll(x, 1, 0))
        xp = jnp.where(row == tt - 1, next_row, pltpu.roll(x, tt - 1, 0))
        return b_ref[...] + xm * w_ref[0:1, :] + x * w_ref[1:2, :] + xp * w_ref[2:3, :]

    x0_out[...] = conv(x0m, x0p, x0n, w0, b0)
    u_out[...] = conv(vm, vp, vn, w2, b2) * conv(x1m, x1p, x1n, w1, b1)


def _hyena_pre(p, conv_w, conv_b, groups, width, *, tt=512, ct=512):
    m = p.shape[0]
    tt = min(tt, min(l for _, _, l in groups))
    ct = min(ct, width)
    nc = width // ct
    first_tiles = tuple((row0 + s * l) // tt for row0, nseq, l in groups for s in range(nseq))
    last_tiles = tuple((row0 + (s + 1) * l) // tt - 1 for row0, nseq, l in groups for s in range(nseq))
    hb = tt // SUBLANES
    nhb = m // SUBLANES

    def main(sec):
        return pl.BlockSpec((tt, ct), lambda i, j: (i, sec * nc + j))

    def prev(sec):
        return pl.BlockSpec((SUBLANES, ct), lambda i, j: (jnp.maximum(i * hb - 1, 0), sec * nc + j))

    def nxt(sec):
        return pl.BlockSpec((SUBLANES, ct), lambda i, j: (jnp.minimum((i + 1) * hb, nhb - 1), sec * nc + j))

    def wspec(sec):
        return pl.BlockSpec((HYENA_SHORT, ct), lambda i, j: (0, sec * nc + j))

    def bspec(sec):
        return pl.BlockSpec((1, ct), lambda i, j: (0, sec * nc + j))

    in_specs = []
    for sec in range(3):
        in_specs += [main(sec), prev(sec), nxt(sec)]
    in_specs += [wspec(0), wspec(1), wspec(2), bspec(0), bspec(1), bspec(2)]
    out_spec = pl.BlockSpec((tt, ct), lambda i, j: (i, j))
    cb = conv_b.reshape(1, -1)
    return pl.pallas_call(
        functools.partial(_hyena_pre_kernel, tt=tt, first_tiles=first_tiles, last_tiles=last_tiles),
        grid=(m // tt, nc),
        in_specs=in_specs,
        out_specs=[out_spec, out_spec],
        out_shape=[jax.ShapeDtypeStruct((m, width), F32)] * 2,
        compiler_params=_params("parallel", "arbitrary"),
        name="hyena_short_conv",
    )(*([p] * 9), conv_w, conv_w, conv_w, cb, cb, cb)


def _hyena_positions(seq_len, tr):
    pos = np.arange(seq_len, dtype=np.float64)[:, None]
    t = pos / max(seq_len - 1, 1)
    w = (2.0 * math.pi / seq_len) * pos
    bands = np.linspace(1e-4, HYENA_BANDS - 1, HYENA_BANDS, dtype=np.float64)[None, :]
    z = np.concatenate([t, np.cos(bands * w), -np.sin(bands * w)], axis=-1)
    z2 = np.concatenate([z, z[:1], z[:0:-1]], axis=0)
    out = np.zeros((2 * seq_len, 64), np.float32)
    out[:, :HYENA_EMB] = z2
    out = out.reshape(2 * seq_len // tr, 2, tr // 2, 64).transpose(0, 2, 1, 3)
    return jnp.asarray(out.reshape(seq_len, 128))


def _filter_kernel(z_ref, win_ref, b_ref, whid_ref, fr_ref, wout_ref, dl_ref, o_ref, *, seq_len):
    hr, packed = z_ref.shape
    half = packed // 2
    hp = lax.Precision.HIGHEST
    z = z_ref[...]
    fr = fr_ref[...]
    h = jnp.sin(fr * (jnp.dot(z, win_ref[...], precision=hp, preferred_element_type=F32) + b_ref[0:1, :]))
    for j in range(HYENA_INNER):
        h = jnp.sin(fr * (jnp.dot(h, whid_ref[j], precision=hp, preferred_element_type=F32)
                          + b_ref[j + 1:j + 2, :]))
    lane = lax.broadcasted_iota(jnp.int32, h.shape, 1)
    row = 2 * hr * pl.program_id(0) + lax.broadcasted_iota(jnp.int32, (hr, o_ref.shape[1]), 0)
    for part in range(2):
        hm = jnp.where((lane >= half) == bool(part), h, 0.0)
        k = jnp.dot(hm, wout_ref[...], precision=hp, preferred_element_type=F32)
        k = k * jnp.exp(-z[:, part * half:part * half + 1] * dl_ref[...])
        o_ref[part * hr:(part + 1) * hr, :] = jnp.where(row + part * hr == seq_len, 0.0, k)


def _hyena_filter(seq_len, w_in, b, w_hid, freq, w_out, *, tr=512):
    order = w_in.shape[1]
    width = w_out.shape[1] // 2
    tr = min(tr, seq_len)
    nfwd = seq_len // tr
    z2 = _hyena_positions(seq_len, tr)
    eye2 = jnp.eye(2, dtype=F32)
    w_in_p = jnp.kron(eye2, jnp.zeros((64, order), F32).at[:HYENA_EMB].set(w_in))
    w_hid_p = jnp.stack([jnp.kron(eye2, w_hid[j]) for j in range(HYENA_INNER)])
    w_out_p = jnp.concatenate([w_out, w_out], axis=0)
    deltas = np.abs(np.linspace(math.log(HYENA_TARGET) / HYENA_SLOW, math.log(HYENA_TARGET) / HYENA_FAST,
                                width, dtype=np.float64))[None, :].astype(np.float32)
    return pl.pallas_call(
        functools.partial(_filter_kernel, seq_len=seq_len),
        grid=(2 * nfwd,),
        in_specs=[
            pl.BlockSpec((tr // 2, z2.shape[1]), lambda i: (i, 0)),
            pl.BlockSpec(w_in_p.shape, lambda i: (0, 0)),
            pl.BlockSpec((b.shape[0], 2 * order), lambda i: (0, 0)),
            pl.BlockSpec(w_hid_p.shape, lambda i: (0, 0, 0)),
            pl.BlockSpec((1, 2 * order), lambda i: (0, 0)),
            pl.BlockSpec((2 * order, width), lambda i: (0, i // nfwd)),
            pl.BlockSpec((1, width), lambda i: (0, 0)),
        ],
        out_specs=pl.BlockSpec((tr, width), lambda i: (i, 0)),
        out_shape=jax.ShapeDtypeStruct((2 * seq_len, width), F32),
        compiler_params=_params("parallel"),
        name="hyena_filter",
    )(z2, w_in_p, jnp.tile(b, (1, 2)), w_hid_p, jnp.tile(freq.reshape(1, order), (1, 2)), w_out_p,
      jnp.asarray(deltas))


def _dft_constants(seq_len):
    n = 2 * seq_len
    n2 = FFT_INNER
    n1 = n // n2
    a1 = 2.0 * np.pi * np.outer(np.arange(n1), np.arange(n1)) / n1
    f1 = np.concatenate([np.cos(a1), -np.sin(a1)], axis=0)
    f3 = np.concatenate([np.cos(a1), -np.sin(a1)], axis=1)[:n1 // 2]
    a2 = 2.0 * np.pi * np.outer(np.arange(n2), np.arange(n2)) / n2
    gr, gi = np.cos(a2), -np.sin(a2)
    g_fwd = np.block([[gr, -gi], [gi, gr]])
    g_inv = np.block([[gr, gi], [-gi, gr]])
    at = 2.0 * np.pi * np.outer(np.arange(n1), np.arange(n2)) / n
    tw_r = np.cos(at)[:, :, None]
    tw_i = -np.sin(at)[:, :, None]
    c = lambda a, dt: jnp.asarray(a, dt)
    return dict(n1=n1, n2=n2, f1_full=c(f1, BF16), f1_half=c(f1[:, :n1 // 2], BF16), f3=c(f3, BF16),
                g_fwd=c(g_fwd, BF16), g_inv=c(g_inv, BF16), tw_r=c(tw_r, F32), tw_i=c(tw_i, F32))


def _dft_outer_fwd_kernel(f_ref, x_ref, o_ref):
    f = f_ref[...]
    for s in range(x_ref.shape[1]):
        o_ref[:, s, :] = jnp.dot(f, x_ref[:, s, :].astype(BF16), preferred_element_type=F32)


def _dft_outer_forward(fmat, x, seq0, nseq, *, ct=1024):
    _, k, n2, c = x.shape
    mo = fmat.shape[0]
    ct = min(ct, c)
    return pl.pallas_call(
        _dft_outer_fwd_kernel,
        grid=(nseq, n2 // SUBLANES, c // ct),
        in_specs=[
            pl.BlockSpec((mo, k), lambda b, s, j: (0, 0)),
            pl.BlockSpec((None, k, SUBLANES, ct), lambda b, s, j: (seq0 + b, 0, s, j)),
        ],
        out_specs=pl.BlockSpec((None, mo, SUBLANES, ct), lambda b, s, j: (b, 0, s, j)),
        out_shape=jax.ShapeDtypeStruct((nseq, mo, n2, c), F32),
        compiler_params=_params("parallel", "parallel", "parallel"),
        name="dft_outer_forward",
    )(fmat, x)


def _twiddle_dft(ar_ref, ai_ref, twr_ref, twi_ref, g_ref):
    n2 = ar_ref.shape[0]
    ar, ai = ar_ref[...], ai_ref[...]
    tr, ti = twr_ref[...], twi_ref[...]
    z = jnp.concatenate([ar * tr - ai * ti, ar * ti + ai * tr], axis=0).astype(BF16)
    x = jnp.dot(g_ref[...], z, preferred_element_type=F32)
    return x[:n2], x[n2:]


def _spectrum_kernel(ar_ref, ai_ref, twr_ref, twi_ref, g_ref, kr_ref, ki_ref, *, scale):
    xr, xi = _twiddle_dft(ar_ref, ai_ref, twr_ref, twi_ref, g_ref)
    kr_ref[...] = xr * scale
    ki_ref[...] = xi * scale


def _fft_mid_kernel(ar_ref, ai_ref, twr_ref, twi_ref, g_ref, gi_ref, kr_ref, ki_ref, br_ref, bi_ref):
    n2 = ar_ref.shape[0]
    xr, xi = _twiddle_dft(ar_ref, ai_ref, twr_ref, twi_ref, g_ref)
    kr, ki = kr_ref[...], ki_ref[...]
    y = jnp.concatenate([xr * kr - xi * ki, xr * ki + xi * kr], axis=0).astype(BF16)
    b = jnp.dot(gi_ref[...], y, preferred_element_type=F32)
    br, bi = b[:n2], b[n2:]
    tr, ti = twr_ref[...], twi_ref[...]
    br_ref[...] = br * tr + bi * ti
    bi_ref[...] = bi * tr - br * ti


def _fft_mid(a, consts, kf=None, *, ct=1024):
    nb, _, n2, c = a.shape
    n1 = consts["n1"]
    ct = min(ct, c)
    grid = (nb, n1, c // ct)
    blk = lambda im: pl.BlockSpec((None, None, n2, ct), lambda b, k, j: (b, im * n1 + k, 0, j))
    tw = pl.BlockSpec((None, n2, 1), lambda b, k, j: (k, 0, 0))
    gs = pl.BlockSpec((2 * n2, 2 * n2), lambda b, k, j: (0, 0))
    half_spec = pl.BlockSpec((None, None, n2, ct), lambda b, k, j: (b, k, 0, j))
    half_shape = jax.ShapeDtypeStruct((nb, n1, n2, c), F32)
    if kf is None:
        return pl.pallas_call(
            functools.partial(_spectrum_kernel, scale=1.0 / (n1 * n2)),
            grid=grid,
            in_specs=[blk(0), blk(1), tw, tw, gs],
            out_specs=[half_spec, half_spec],
            out_shape=[half_shape, half_shape],
            compiler_params=_params("parallel", "parallel", "parallel"),
            name="dft_filter_spectrum",
        )(a, a, consts["tw_r"], consts["tw_i"], consts["g_fwd"])
    kblk = lambda: pl.BlockSpec((None, None, n2, ct), lambda b, k, j: (0, k, 0, j))
    return pl.pallas_call(
        _fft_mid_kernel,
        grid=grid,
        in_specs=[blk(0), blk(1), tw, tw, gs, gs, kblk(), kblk()],
        out_specs=[half_spec, half_spec],
        out_shape=[half_shape, half_shape],
        compiler_params=_params("parallel", "parallel", "parallel"),
        name="dft_inner_filter",
    )(a, a, consts["tw_r"], consts["tw_i"], consts["g_fwd"], consts["g_inv"], kf[0], kf[1])


def _dft_outer_inv_kernel(f_ref, br_ref, bi_ref, u_ref, x0_ref, bias_ref, *rest):
    o_ref = rest[-1]
    f = f_ref[...]
    bias = bias_ref[...]
    for s in range(u_ref.shape[1]):
        b = jnp.concatenate([br_ref[:, s, :], bi_ref[:, s, :]], axis=0).astype(BF16)
        y = jnp.dot(f, b, preferred_element_type=F32)
        o_ref[:, s, :] = x0_ref[:, s, :] * (y + u_ref[:, s, :] * bias)


def _dft_outer_inverse(f3, br, bi, u, x0, bias, seq0, prev, *, ct=1024):
    nseq, n1, n2, c = br.shape
    mo = f3.shape[0]
    ct = min(ct, c)
    spec_in = pl.BlockSpec((None, n1, SUBLANES, ct), lambda b, s, j: (b, 0, s, j))
    spec_tok = pl.BlockSpec((None, mo, SUBLANES, ct), lambda b, s, j: (seq0 + b, 0, s, j))
    in_specs = [pl.BlockSpec(f3.shape, lambda b, s, j: (0, 0)), spec_in, spec_in, spec_tok, spec_tok,
                pl.BlockSpec((1, ct), lambda b, s, j: (0, j))]
    args = [f3, br, bi, u, x0, bias.reshape(1, c)]
    aliases = _alias_prev(in_specs, args, None if prev is None else prev.reshape(u.shape))
    return pl.pallas_call(
        _dft_outer_inv_kernel,
        grid=(nseq, n2 // SUBLANES, c // ct),
        in_specs=in_specs,
        out_specs=spec_tok,
        out_shape=jax.ShapeDtypeStruct(u.shape, F32),
        input_output_aliases=aliases,
        compiler_params=_params("parallel", "parallel", "parallel"),
        name="dft_outer_inverse",
    )(*args)


def _hyena_long_conv_group(u, x0, k_two, bias, row0, nseq, seq_len, prev):
    m, c = u.shape
    consts = _dft_constants(seq_len)
    n1, n2 = consts["n1"], consts["n2"]
    ak = _dft_outer_forward(consts["f1_full"], k_two.reshape(1, n1, n2, c), 0, 1)
    kfr, kfi = _fft_mid(ak, consts)
    tok_view = (m // seq_len, n1 // 2, n2, c)
    seq0 = row0 // seq_len
    a = _dft_outer_forward(consts["f1_half"], u.reshape(tok_view), seq0, nseq)
    br, bi = _fft_mid(a, consts, kf=(kfr, kfi))
    y = _dft_outer_inverse(consts["f3"], br, bi, u.reshape(tok_view), x0.reshape(tok_view), bias, seq0, prev)
    return y.reshape(m, c)


def _softplus(z):
    return jnp.maximum(z, 0.0) + jnp.log(1.0 + jnp.exp(-jnp.abs(z)))


def _lru_kernel(pl_ref, pg_ref, cw_ref, cb_ref, wa_ref, ba_ref, wx_ref, bx_ref, lam_ref, *rest, tchunk):
    o_ref, xp_ref, hs_ref = rest[-3:]
    seq_len, cw = pl_ref.shape
    nchunk = seq_len // tchunk
    halo = SUBLANES
    xp_ref[0:halo, :] = jnp.zeros((halo, cw), F32)
    xp_ref[halo + seq_len:2 * halo + seq_len, :] = jnp.zeros((halo, cw), F32)
    xp_ref[halo:halo + seq_len, :] = pl_ref[...]

    row = lax.broadcasted_iota(jnp.int32, (tchunk, cw), 0)
    nsteps = int(math.log2(tchunk))
    neg_c_softplus = -LRU_C * _softplus(-lam_ref[...])

    def gates(c, direction):
        base = pl.multiple_of(c * tchunk, tchunk)
        xw = xp_ref[pl.ds(base, tchunk + 2 * halo), :]
        ext = tchunk + 2 * halo
        xb = cb_ref[...]
        for j in range(LRU_CONV):
            shift = (LRU_CONV // 2 - j) % ext
            tap = xw if shift == 0 else pltpu.roll(xw, shift, 0)
            xb = xb + tap[halo:halo + tchunk, :] * cw_ref[j:j + 1, :]
        xb16 = xb.astype(BF16)
        r = jax.nn.sigmoid(jnp.dot(xb16, wa_ref[direction], preferred_element_type=F32)
                           + ba_ref[direction:direction + 1, :])
        i = jax.nn.sigmoid(jnp.dot(xb16, wx_ref[direction], preferred_element_type=F32)
                           + bx_ref[direction:direction + 1, :])
        log_a = r * neg_c_softplus[direction:direction + 1, :]
        a = jnp.exp(log_a)
        gain = jnp.sqrt(-jnp.tanh(log_a) * (a * a + 1.0))
        return a, gain * (i * xb)

    def scan(a, b, reverse):
        for s in range(nsteps):
            d = 1 << s
            if reverse:
                keep = row < tchunk - d
                a_sh = pltpu.roll(a, tchunk - d, 0)
                b_sh = pltpu.roll(b, tchunk - d, 0)
            else:
                keep = row >= d
                a_sh = pltpu.roll(a, d, 0)
                b_sh = pltpu.roll(b, d, 0)
            b = jnp.where(keep, a * b_sh + b, b)
            a = jnp.where(keep, a * a_sh, a)
        return a, b

    def fwd_body(c, h):
        a, b = gates(c, 0)
        a, b = scan(a, b, False)
        hc = a * h + b
        hs_ref[pl.ds(pl.multiple_of(c * tchunk, tchunk), tchunk), :] = hc
        return hc[tchunk - 1:tchunk, :]

    lax.fori_loop(0, nchunk, fwd_body, jnp.zeros((1, cw), F32), unroll=min(4, nchunk))

    def rev_body(cc, h):
        c = nchunk - 1 - cc
        a, b = gates(c, 1)
        a, b = scan(a, b, True)
        hc = a * h + b
        sl = pl.ds(pl.multiple_of(c * tchunk, tchunk), tchunk)
        o_ref[sl, :] = (hs_ref[sl, :] + hc) * jax.nn.gelu(pg_ref[sl, :])
        return hc[0:1, :]

    lax.fori_loop(0, nchunk, rev_body, jnp.zeros((1, cw), F32), unroll=min(4, nchunk))


def _rg_lru_group(p, col_gate, col_x, conv_w, conv_b, wa, ba, wx, bx, lam, row0, nseq, seq_len, prev, *,
                  lead=(), tchunk=128):
    m = p.shape[0]
    nblk = wa.shape[-3]
    cw = wa.shape[-1]
    tchunk = min(tchunk, seq_len)
    sb0 = row0 // seq_len
    gb0 = col_gate // cw
    xb0 = col_x // cw
    vec = lambda rows: pl.BlockSpec((rows, cw), lambda b, n: (0, n))
    mat = _stacked_spec(lead, (2, None, cw, cw), lambda b, n: (0, n, 0, 0))
    in_specs = [
        pl.BlockSpec((seq_len, cw), lambda b, n: (sb0 + b, xb0 + n)),
        pl.BlockSpec((seq_len, cw), lambda b, n: (sb0 + b, gb0 + n)),
        vec(LRU_CONV), vec(1), mat, vec(2), mat, vec(2), vec(2),
    ]
    args = [p, p, conv_w, conv_b.reshape(1, -1), wa, ba, wx, bx, lam]
    aliases = _alias_prev(in_specs, args, prev)
    return pl.pallas_call(
        functools.partial(_lru_kernel, tchunk=tchunk),
        grid=(nseq, nblk),
        in_specs=in_specs,
        out_specs=pl.BlockSpec((seq_len, cw), lambda b, n: (sb0 + b, n)),
        out_shape=jax.ShapeDtypeStruct((m, nblk * cw), F32),
        input_output_aliases=aliases,
        scratch_shapes=[pltpu.VMEM((seq_len + 2 * SUBLANES, cw), F32), pltpu.VMEM((seq_len, cw), F32)],
        compiler_params=_params("parallel", "parallel"),
        name="rg_lru",
    )(*args)


def kernel(x_prompt, x_sample, ffn_norm, ffn_w1, ffn_w3, ffn_w2, mix_norm, final_norm, even_w_in, hyena_conv_w, hyena_conv_b, hyena_filt_w_in, hyena_filt_b, hyena_filt_w_hid, hyena_filt_freq, hyena_filt_w_out, hyena_bias, lru_conv_w, lru_conv_b, lru_wa, lru_ba, lru_wx, lru_bx, lru_lambda, even_w_out, attn_w_qkv, attn_lambda, attn_subln, attn_w_o):
    d = x_prompt.shape[-1]
    depth = ffn_norm.shape[0]
    groups = []
    row0 = 0
    for xg in (x_prompt, x_sample):
        groups.append((row0, xg.shape[0], xg.shape[1]))
        row0 += xg.shape[0] * xg.shape[1]
    groups = tuple(groups)
    m = row0
    inputs = (x_prompt, x_sample)

    hw = hyena_bias.shape[1]
    head_dim = d // (2 * ATT_HEADS)
    rope = _rope_tables(max(l for _, _, l in groups), head_dim)
    w1, w3, w2 = ffn_w1.astype(BF16), ffn_w3.astype(BF16), ffn_w2.astype(BF16)
    w_in, w_out = even_w_in.astype(BF16), even_w_out.astype(BF16)
    w_qkv, w_o = _permute_qk_columns(attn_w_qkv.astype(BF16), d, head_dim), attn_w_o.astype(BF16)
    wa, wx = lru_wa.astype(BF16), lru_wx.astype(BF16)
    lru_cols = (3 * hw, 3 * hw + lru_wa.shape[-3] * lru_wa.shape[-1])

    x = None
    for i in range(depth):
        j = i // 2
        if i == 0:
            for (row0, _, _), xg in zip(groups, inputs):
                x = _ffn(xg.reshape(-1, d), ffn_norm[i, 0], w1, w3, w2, lead=(i, 0), out_rows=m, out_row0=row0,
                         prev=x)
        else:
            x = _ffn(x, ffn_norm[i, 0], w1, w3, w2, lead=(i, 0))
        if i % 2 == 0:
            p = _norm_matmul(x, mix_norm[i], w_in, lead=(j,))
            x0, u = _hyena_pre(p, hyena_conv_w[j], hyena_conv_b[j], groups, hw)
            yh = yl = None
            for row0, nseq, seq_len in groups:
                k_two = _hyena_filter(seq_len, hyena_filt_w_in[j], hyena_filt_b[j], hyena_filt_w_hid[j],
                                      hyena_filt_freq[j], hyena_filt_w_out[j])
                yh = _hyena_long_conv_group(u, x0, k_two, hyena_bias[j], row0, nseq, seq_len, yh)
                yl = _rg_lru_group(p, *lru_cols, lru_conv_w[j], lru_conv_b[j], wa, lru_ba[j], wx, lru_bx[j],
                                   lru_lambda[j], row0, nseq, seq_len, yl, lead=(j,))
            x = _proj_residual(x, [yh, yl], w_out, lead=(j,))
        else:
            lambda_init = 0.8 - 0.6 * math.exp(-0.3 * i)
            qkv = _qkv(x, mix_norm[i], w_qkv, rope, groups, lead=(j,))
            o = None
            for row0, nseq, seq_len in groups:
                o = _diff_attention_group(qkv, attn_lambda[j], attn_subln[j], row0, nseq, seq_len, lambda_init, o)
            x = _proj_residual(x, [o], w_o, lead=(j,))
        if i + 1 < depth:
            x = _ffn(x, ffn_norm[i, 1], w1, w3, w2, lead=(i, 1))

    outs = []
    for (row0, nseq, seq_len), xg in zip(groups, inputs):
        y = _ffn(x, ffn_norm[depth - 1, 1], w1, w3, w2, lead=(depth - 1, 1), in_row0=row0, nrows=nseq * seq_len,
                 final_g=final_norm)
        outs.append(y.reshape(xg.shape))
    return tuple(outs)
```

```python
import functools
import math

import numpy as np
import jax
import jax.numpy as jnp
from jax import lax
from jax.experimental import pallas as pl
from jax.experimental.pallas import tpu as pltpu

F32 = jnp.float32
BF16 = jnp.bfloat16

NORM_EPS = 1e-6
HYENA_SHORT = 3
HYENA_EMB = 33
HYENA_BANDS = (HYENA_EMB - 1) // 2
HYENA_INNER = 2
HYENA_TARGET = 1e-2
HYENA_FAST = 0.3
HYENA_SLOW = 1.5
LRU_BLOCKS = 8
LRU_CONV = 4
LRU_C = 8.0
ATT_HEADS = 16
ROPE_THETA = 10000.0
LOG2E = 1.4426950408889634

LANES = 128
SUBLANES = 8
VMEM_LIMIT_BYTES = 56 * 1024 * 1024
FFT_INNER = 128


def _params(*sem):
    return pltpu.CompilerParams(dimension_semantics=sem, vmem_limit_bytes=VMEM_LIMIT_BYTES)


def _rms(x, g):
    ms = jnp.mean(x * x, axis=-1, keepdims=True)
    return x * lax.rsqrt(ms + NORM_EPS) * g


def _stacked_spec(lead, block, index):
    lead = tuple(lead)
    return pl.BlockSpec((None,) * len(lead) + tuple(block), lambda *g: lead + tuple(index(*g)))


def _alias_prev(in_specs, args, prev):
    if prev is None:
        return {}
    in_specs.append(pl.BlockSpec(memory_space=pl.ANY))
    args.append(prev)
    return {len(args) - 1: 0}


def _ffn_kernel(x_ref, g_ref, w1_ref, w3_ref, w2_ref, *rest, final_norm):
    gf_ref = rest[0] if final_norm else None
    o_ref, xn_ref = rest[-2:]
    f = pl.program_id(1)

    @pl.when(f == 0)
    def _():
        x = x_ref[...]
        xn_ref[...] = _rms(x, g_ref[...]).astype(BF16)
        o_ref[...] = x

    xn = xn_ref[...]
    a = jnp.dot(xn, w1_ref[...], preferred_element_type=F32)
    b = jnp.dot(xn, w3_ref[...], preferred_element_type=F32)
    h = (0.5 * a * jax.nn.sigmoid(a) * b).astype(BF16)
    o_ref[...] += jnp.dot(h, w2_ref[...], preferred_element_type=F32)

    if final_norm:
        @pl.when(f == pl.num_programs(1) - 1)
        def _():
            o_ref[...] = _rms(o_ref[...], gf_ref[...])


def _ffn(x, g, w1, w3, w2, *, lead=(), in_row0=0, nrows=None, out_rows=None, out_row0=0, prev=None,
         final_g=None, tm=1024, tf=512):
    d = x.shape[1]
    mx = x.shape[0] if nrows is None else nrows
    ff = w1.shape[-1]
    tm = min(tm, mx)
    tf = min(tf, ff)
    out_rows = mx if out_rows is None else out_rows
    rb0 = out_row0 // tm
    ib0 = in_row0 // tm
    in_specs = [
        pl.BlockSpec((tm, d), lambda i, f: (ib0 + i, 0)),
        pl.BlockSpec((1, d), lambda i, f: (0, 0)),
        _stacked_spec(lead, (d, tf), lambda i, f: (0, f)),
        _stacked_spec(lead, (d, tf), lambda i, f: (0, f)),
        _stacked_spec(lead, (tf, d), lambda i, f: (f, 0)),
    ]
    args = [x, g.reshape(1, d), w1, w3, w2]
    if final_g is not None:
        in_specs.append(pl.BlockSpec((1, d), lambda i, f: (0, 0)))
        args.append(final_g.reshape(1, d))
    aliases = _alias_prev(in_specs, args, prev)
    return pl.pallas_call(
        functools.partial(_ffn_kernel, final_norm=final_g is not None),
        grid=(mx // tm, ff // tf),
        in_specs=in_specs,
        out_specs=pl.BlockSpec((tm, d), lambda i, f: (rb0 + i, 0)),
        out_shape=jax.ShapeDtypeStruct((out_rows, d), F32),
        input_output_aliases=aliases,
        scratch_shapes=[pltpu.VMEM((tm, d), BF16)],
        compiler_params=_params("parallel", "arbitrary"),
        name="ffn",
    )(*args)


def _norm_matmul_kernel(x_ref, g_ref, w_ref, o_ref, xn_ref):
    @pl.when(pl.program_id(1) == 0)
    def _():
        xn_ref[...] = _rms(x_ref[...], g_ref[...]).astype(BF16)

    o_ref[...] = jnp.dot(xn_ref[...], w_ref[...], preferred_element_type=F32).astype(o_ref.dtype)


def _norm_matmul(x, g, w, *, lead=(), out_dtype=F32, tm=1024, tn=1024):
    m, d = x.shape
    n = w.shape[-1]
    tm = min(tm, m)
    tn = min(tn, n)
    return pl.pallas_call(
        _norm_matmul_kernel,
        grid=(m // tm, n // tn),
        in_specs=[
            pl.BlockSpec((tm, d), lambda i, j: (i, 0)),
            pl.BlockSpec((1, d), lambda i, j: (0, 0)),
            _stacked_spec(lead, (d, tn), lambda i, j: (0, j)),
        ],
        out_specs=pl.BlockSpec((tm, tn), lambda i, j: (i, j)),
        out_shape=jax.ShapeDtypeStruct((m, n), out_dtype),
        scratch_shapes=[pltpu.VMEM((tm, d), BF16)],
        compiler_params=_params("parallel", "arbitrary"),
        name="norm_matmul",
    )(x, g.reshape(1, d), w)


def _rope_tables(lmax, head_dim):
    half = head_dim // 2
    inv = ROPE_THETA ** (-np.arange(0, head_dim, 2, dtype=np.float64) / head_dim)
    ang = np.arange(lmax, dtype=np.float64)[:, None] * inv[None, :]
    ang = np.tile(ang, (1, LANES // half))
    second = (np.arange(LANES) // (LANES // 2)) == 1
    cos = np.cos(ang)
    sin = np.where(second[None, :], np.sin(ang), -np.sin(ang))
    q_scale = head_dim ** -0.5 * LOG2E
    stack = lambda q, k, v: jnp.asarray(np.stack([q, k, v]), F32)
    return (stack(cos * q_scale, cos, np.ones_like(cos)), stack(sin * q_scale, sin, np.zeros_like(sin)))


def _permute_qk_columns(w, d, head_dim):
    lead = w.shape[:-1]
    qk = w[..., :2 * d].reshape(*lead, 2 * d // (2 * head_dim), 2, 2, head_dim // 2)
    qk = jnp.swapaxes(qk, -3, -2).reshape(*lead, 2 * d)
    return jnp.concatenate([qk, w[..., 2 * d:]], axis=-1)


def _qkv_kernel(x_ref, g_ref, w_ref, cos_ref, sin_ref, o_ref, xn_ref):
    @pl.when(pl.program_id(1) == 0)
    def _():
        xn_ref[...] = _rms(x_ref[...], g_ref[...]).astype(BF16)

    res = jnp.dot(xn_ref[...], w_ref[...], preferred_element_type=F32)
    cos, sin = cos_ref[...], sin_ref[...]
    for c in range(res.shape[1] // LANES):
        blk = res[:, c * LANES:(c + 1) * LANES]
        rot = blk * cos + pltpu.roll(blk, LANES // 2, 1) * sin
        o_ref[:, c * LANES:(c + 1) * LANES] = rot.astype(o_ref.dtype)


def _qkv(x, g, w, tables, groups, *, lead=(), tm=1024, tn=512):
    m, d = x.shape
    n = w.shape[-1]
    tm = min(tm, min(l for _, _, l in groups))
    n_sec = d // tn
    cos, sin = tables

    def table_block(i, j):
        blk = 0
        for row0, nseq, l in groups:
            inside = jnp.logical_and(i * tm >= row0, i * tm < row0 + nseq * l)
            blk = jnp.where(inside, ((i * tm - row0) % l) // tm, blk)
        return (j // n_sec, blk, 0)

    table_spec = pl.BlockSpec((None, tm, LANES), table_block)
    return pl.pallas_call(
        _qkv_kernel,
        grid=(m // tm, n // tn),
        in_specs=[
            pl.BlockSpec((tm, d), lambda i, j: (i, 0)),
            pl.BlockSpec((1, d), lambda i, j: (0, 0)),
            _stacked_spec(lead, (d, tn), lambda i, j: (0, j)),
            table_spec, table_spec,
        ],
        out_specs=pl.BlockSpec((tm, tn), lambda i, j: (i, j)),
        out_shape=jax.ShapeDtypeStruct((m, n), BF16),
        scratch_shapes=[pltpu.VMEM((tm, d), BF16)],
        compiler_params=_params("parallel", "arbitrary"),
        name="qkv_rope",
    )(x, g.reshape(1, d), w, cos, sin)


def _attn_kernel(lam_ref, g_ref, q_ref, k_ref, v_ref, *rest, tk, tq, lambda_init):
    o_ref, vt_ref, acc_ref, sa_ref, sb_ref = rest[-5:]
    seq_len = k_ref.shape[0]
    nchunk = seq_len // tk
    nsub = q_ref.shape[0] // tq

    @pl.when(pl.program_id(2) == 0)
    def _():
        for c in range(nchunk):
            vt_ref[c] = v_ref[c * tk:(c + 1) * tk, :].astype(F32).T.astype(BF16)

    comp1 = (lax.broadcasted_iota(jnp.int32, (tq, q_ref.shape[1]), 1) // (q_ref.shape[1] // 4)) % 2 == 1
    lv = lam_ref[...]
    lam = (jnp.exp(jnp.sum(lv[0:1] * lv[1:2], axis=1, keepdims=True))
           - jnp.exp(jnp.sum(lv[2:3] * lv[3:4], axis=1, keepdims=True)) + lambda_init)

    def rows(j):
        return pl.ds(j * tq if isinstance(j, int) else pl.multiple_of(j * tq, tq), tq)

    def q_parts(j):
        q = q_ref[rows(j), :]
        zero = jnp.zeros_like(q)
        return jnp.where(comp1, zero, q), jnp.where(comp1, q, zero)

    def score_step(c, ms, qs, s_out):
        k = k_ref[pl.ds(pl.multiple_of(c * tk, tk), tk), :]
        new = []
        for comp in range(2):
            s = lax.dot_general(k, qs[comp], (((1,), (1,)), ((), ())), preferred_element_type=F32)
            s_out[comp, c] = s
            new.append(jnp.maximum(ms[comp], jnp.max(s, axis=0, keepdims=True)))
        return tuple(new)

    def pv_step(c, ls, ms, s_in):
        vt = vt_ref[c]
        new = []
        for comp in range(2):
            p = jnp.exp2(s_in[comp, c] - ms[comp])
            new.append(ls[comp] + jnp.sum(p, axis=0, keepdims=True))
            acc_ref[comp] += jnp.dot(vt, p.astype(BF16), preferred_element_type=F32)
        return tuple(new)

    def finalize(j, ls):
        ot = acc_ref[0] * (1.0 / ls[0]) - acc_ref[1] * (lam * (1.0 / ls[1]))
        ms2 = jnp.mean(ot * ot, axis=0, keepdims=True)
        ot = ot * lax.rsqrt(ms2 + NORM_EPS) * g_ref[...] * (1.0 - lambda_init)
        o_ref[rows(j), :] = ot.T.astype(o_ref.dtype)

    neg = (jnp.full((1, tq), -jnp.inf, F32),) * 2
    zero_rows = (jnp.zeros((1, tq), F32),) * 2

    qs0 = q_parts(0)
    ms0 = lax.fori_loop(0, nchunk, lambda c, m: score_step(c, m, qs0, sa_ref), neg, unroll=min(8, nchunk))

    def fused(j, ms, s_in, s_out):
        acc_ref[...] = jnp.zeros_like(acc_ref)
        qn = q_parts(j + 1)

        def both(c, carry):
            ls, mn = carry
            return pv_step(c, ls, ms, s_in), score_step(c, mn, qn, s_out)

        ls, ms_next = lax.fori_loop(0, nchunk, both, (zero_rows, neg), unroll=min(8, nchunk))
        finalize(j, ls)
        return ms_next

    def pair(jj, ms):
        return fused(2 * jj + 1, fused(2 * jj, ms, sa_ref, sb_ref), sb_ref, sa_ref)

    ms_last = lax.fori_loop(0, (nsub - 1) // 2, pair, ms0)
    if (nsub - 1) % 2:
        ms_last = fused(nsub - 2, ms_last, sa_ref, sb_ref)
    last = nsub - 1
    s_last = sb_ref if last % 2 else sa_ref
    acc_ref[...] = jnp.zeros_like(acc_ref)
    ls = lax.fori_loop(0, nchunk, lambda c, l: pv_step(c, l, ms_last, s_last), zero_rows, unroll=min(8, nchunk))
    finalize(last, ls)


def _diff_attention_group(qkv, lam_vec, subln_g, row0, nseq, seq_len, lambda_init, prev, *, tq=256, nsub=16,
                          tk=512):
    m = qkv.shape[0]
    d = qkv.shape[1] // 3
    hw = d // ATT_HEADS
    tq = min(tq, seq_len)
    tk = min(tk, seq_len)
    bq = min(nsub * tq, seq_len)
    nq = seq_len // bq
    qb0 = row0 // bq
    kb0 = row0 // seq_len
    in_specs = [
        pl.BlockSpec(lam_vec.shape, lambda b, h, i: (0, 0)),
        pl.BlockSpec((hw, 1), lambda b, h, i: (0, 0)),
        pl.BlockSpec((bq, hw), lambda b, h, i: (qb0 + b * nq + i, h)),
        pl.BlockSpec((seq_len, hw), lambda b, h, i: (kb0 + b, ATT_HEADS + h)),
        pl.BlockSpec((seq_len, hw), lambda b, h, i: (kb0 + b, 2 * ATT_HEADS + h)),
    ]
    args = [lam_vec, subln_g.reshape(hw, 1), qkv, qkv, qkv]
    aliases = _alias_prev(in_specs, args, prev)
    return pl.pallas_call(
        functools.partial(_attn_kernel, tk=tk, tq=tq, lambda_init=lambda_init),
        grid=(nseq, ATT_HEADS, nq),
        in_specs=in_specs,
        out_specs=pl.BlockSpec((bq, hw), lambda b, h, i: (qb0 + b * nq + i, h)),
        out_shape=jax.ShapeDtypeStruct((m, d), BF16),
        input_output_aliases=aliases,
        scratch_shapes=[
            pltpu.VMEM((seq_len // tk, hw, tk), BF16),
            pltpu.VMEM((2, hw, tq), F32),
            pltpu.VMEM((2, seq_len // tk, tk, tq), F32),
            pltpu.VMEM((2, seq_len // tk, tk, tq), F32),
        ],
        compiler_params=_params("parallel", "parallel", "arbitrary"),
        name="diff_attention",
    )(*args)


def _proj_res_kernel(*refs, nparts):
    res_ref = refs[0]
    a_refs = refs[1:1 + nparts]
    w_refs = refs[1 + nparts:1 + 2 * nparts]
    o_ref = refs[1 + 2 * nparts]
    acc = res_ref[...]
    for a_ref, w_ref in zip(a_refs, w_refs):
        acc = acc + jnp.dot(a_ref[...].astype(BF16), w_ref[...], preferred_element_type=F32)
    o_ref[...] = acc


def _proj_residual(res, parts, w, *, lead=(), tm=512, tn=2048):
    m, n = res.shape
    tm = min(tm, m)
    tn = min(tn, n)
    nparts = len(parts)
    kp = w.shape[-2] // nparts
    in_specs = [pl.BlockSpec((tm, tn), lambda i, j: (i, j))]
    in_specs += [pl.BlockSpec((tm, kp), lambda i, j: (i, 0)) for _ in parts]
    in_specs += [_stacked_spec(lead, (kp, tn), functools.partial(lambda i, j, p: (p, j), p=p))
                 for p in range(nparts)]
    return pl.pallas_call(
        functools.partial(_proj_res_kernel, nparts=nparts),
        grid=(m // tm, n // tn),
        in_specs=in_specs,
        out_specs=pl.BlockSpec((tm, tn), lambda i, j: (i, j)),
        out_shape=jax.ShapeDtypeStruct((m, n), F32),
        compiler_params=_params("parallel", "arbitrary"),
        name="proj_residual",
    )(res, *parts, *([w] * nparts))


def _is_any(i, values):
    hit = i == values[0]
    for v in values[1:]:
        hit = jnp.logical_or(hit, i == v)
    return hit


def _hyena_pre_kernel(*refs, tt, first_tiles, last_tiles):
    (x0m, x0p, x0n, x1m, x1p, x1n, vm, vp, vn, w0, w1, w2, b0, b1, b2, x0_out, u_out) = refs
    i = pl.program_id(0)
    first = _is_any(i, first_tiles)
    last = _is_any(i, last_tiles)

    def conv(m_ref, p_ref, n_ref, w_ref, b_ref):
        x = m_ref[...]
        row = lax.broadcasted_iota(jnp.int32, x.shape, 0)
        prev_row = jnp.where(first, 0.0, p_ref[SUBLANES - 1:SUBLANES, :])
        next_row = jnp.where(last, 0.0, n_ref[0:1, :])
        xm = jnp.where(row == 0, prev_row, pltpu.roll(x, 1, 0))
        xp = jnp.where(row == tt - 1, next_row, pltpu.roll(x, tt - 1, 0))
        return b_ref[...] + xm * w_ref[0:1, :] + x * w_ref[1:2, :] + xp * w_ref[2:3, :]

    x0_out[...] = conv(x0m, x0p, x0n, w0, b0)
    u_out[...] = conv(vm, vp, vn, w2, b2) * conv(x1m, x1p, x1n, w1, b1)


def _hyena_pre(p, conv_w, conv_b, groups, width, *, tt=512, ct=512):
    m = p.shape[0]
    tt = min(tt, min(l for _, _, l in groups))
    ct = min(ct, width)
    nc = width // ct
    first_tiles = tuple((row0 + s * l) // tt for row0, nseq, l in groups for s in range(nseq))
    last_tiles = tuple((row0 + (s + 1) * l) // tt - 1 for row0, nseq, l in groups for s in range(nseq))
    hb = tt // SUBLANES
    nhb = m // SUBLANES

    def main(sec):
        return pl.BlockSpec((tt, ct), lambda i, j: (i, sec * nc + j))

    def prev(sec):
        return pl.BlockSpec((SUBLANES, ct), lambda i, j: (jnp.maximum(i * hb - 1, 0), sec * nc + j))

    def nxt(sec):
        return pl.BlockSpec((SUBLANES, ct), lambda i, j: (jnp.minimum((i + 1) * hb, nhb - 1), sec * nc + j))

    def wspec(sec):
        return pl.BlockSpec((HYENA_SHORT, ct), lambda i, j: (0, sec * nc + j))

    def bspec(sec):
        return pl.BlockSpec((1, ct), lambda i, j: (0, sec * nc + j))

    in_specs = []
    for sec in range(3):
        in_specs += [main(sec), prev(sec), nxt(sec)]
    in_specs += [wspec(0), wspec(1), wspec(2), bspec(0), bspec(1), bspec(2)]
    out_spec = pl.BlockSpec((tt, ct), lambda i, j: (i, j))
    cb = conv_b.reshape(1, -1)
    return pl.pallas_call(
        functools.partial(_hyena_pre_kernel, tt=tt, first_tiles=first_tiles, last_tiles=last_tiles),
        grid=(m // tt, nc),
        in_specs=in_specs,
        out_specs=[out_spec, out_spec],
        out_shape=[jax.ShapeDtypeStruct((m, width), F32)] * 2,
        compiler_params=_params("parallel", "arbitrary"),
        name="hyena_short_conv",
    )(*([p] * 9), conv_w, conv_w, conv_w, cb, cb, cb)


def _hyena_positions(seq_len, tr):
    pos = np.arange(seq_len, dtype=np.float64)[:, None]
    t = pos / max(seq_len - 1, 1)
    w = (2.0 * math.pi / seq_len) * pos
    bands = np.linspace(1e-4, HYENA_BANDS - 1, HYENA_BANDS, dtype=np.float64)[None, :]
    z = np.concatenate([t, np.cos(bands * w), -np.sin(bands * w)], axis=-1)
    z2 = np.concatenate([z, z[:1], z[:0:-1]], axis=0)
    out = np.zeros((2 * seq_len, 64), np.float32)
    out[:, :HYENA_EMB] = z2
    out = out.reshape(2 * seq_len // tr, 2, tr // 2, 64).transpose(0, 2, 1, 3)
    return jnp.asarray(out.reshape(seq_len, 128))


def _filter_kernel(z_ref, win_ref, b_ref, whid_ref, fr_ref, wout_ref, dl_ref, o_ref, *, seq_len):
    hr, packed = z_ref.shape
    half = packed // 2
    hp = lax.Precision.HIGHEST
    z = z_ref[...]
    fr = fr_ref[...]
    h = jnp.sin(fr * (jnp.dot(z, win_ref[...], precision=hp, preferred_element_type=F32) + b_ref[0:1, :]))
    for j in range(HYENA_INNER):
        h = jnp.sin(fr * (jnp.dot(h, whid_ref[j], precision=hp, preferred_element_type=F32)
                          + b_ref[j + 1:j + 2, :]))
    lane = lax.broadcasted_iota(jnp.int32, h.shape, 1)
    row = 2 * hr * pl.program_id(0) + lax.broadcasted_iota(jnp.int32, (hr, o_ref.shape[1]), 0)
    for part in range(2):
        hm = jnp.where((lane >= half) == bool(part), h, 0.0)
        k = jnp.dot(hm, wout_ref[...], precision=hp, preferred_element_type=F32)
        k = k * jnp.exp(-z[:, part * half:part * half + 1] * dl_ref[...])
        o_ref[part * hr:(part + 1) * hr, :] = jnp.where(row + part * hr == seq_len, 0.0, k)


def _hyena_filter(seq_len, w_in, b, w_hid, freq, w_out, *, tr=512):
    order = w_in.shape[1]
    width = w_out.shape[1] // 2
    tr = min(tr, seq_len)
    nfwd = seq_len // tr
    z2 = _hyena_positions(seq_len, tr)
    eye2 = jnp.eye(2, dtype=F32)
    w_in_p = jnp.kron(eye2, jnp.zeros((64, order), F32).at[:HYENA_EMB].set(w_in))
    w_hid_p = jnp.stack([jnp.kron(eye2, w_hid[j]) for j in range(HYENA_INNER)])
    w_out_p = jnp.concatenate([w_out, w_out], axis=0)
    deltas = np.abs(np.linspace(math.log(HYENA_TARGET) / HYENA_SLOW, math.log(HYENA_TARGET) / HYENA_FAST,
                                width, dtype=np.float64))[None, :].astype(np.float32)
    return pl.pallas_call(
        functools.partial(_filter_kernel, seq_len=seq_len),
        grid=(2 * nfwd,),
        in_specs=[
            pl.BlockSpec((tr // 2, z2.shape[1]), lambda i: (i, 0)),
            pl.BlockSpec(w_in_p.shape, lambda i: (0, 0)),
            pl.BlockSpec((b.shape[0], 2 * order), lambda i: (0, 0)),
            pl.BlockSpec(w_hid_p.shape, lambda i: (0, 0, 0)),
            pl.BlockSpec((1, 2 * order), lambda i: (0, 0)),
            pl.BlockSpec((2 * order, width), lambda i: (0, i // nfwd)),
            pl.BlockSpec((1, width), lambda i: (0, 0)),
        ],
        out_specs=pl.BlockSpec((tr, width), lambda i: (i, 0)),
        out_shape=jax.ShapeDtypeStruct((2 * seq_len, width), F32),
        compiler_params=_params("parallel"),
        name="hyena_filter",
    )(z2, w_in_p, jnp.tile(b, (1, 2)), w_hid_p, jnp.tile(freq.reshape(1, order), (1, 2)), w_out_p,
      jnp.asarray(deltas))


def _dft_constants(seq_len):
    n = 2 * seq_len
    n2 = FFT_INNER
    n1 = n // n2
    a1 = 2.0 * np.pi * np.outer(np.arange(n1), np.arange(n1)) / n1
    f1 = np.concatenate([np.cos(a1), -np.sin(a1)], axis=0)
    f3 = np.concatenate([np.cos(a1), -np.sin(a1)], axis=1)[:n1 // 2]
    a2 = 2.0 * np.pi * np.outer(np.arange(n2), np.arange(n2)) / n2
    gr, gi = np.cos(a2), -np.sin(a2)
    g_fwd = np.block([[gr, -gi], [gi, gr]])
    g_inv = np.block([[gr, gi], [-gi, gr]])
    at = 2.0 * np.pi * np.outer(np.arange(n1), np.arange(n2)) / n
    tw_r = np.cos(at)[:, :, None]
    tw_i = -np.sin(at)[:, :, None]
    c = lambda a, dt: jnp.asarray(a, dt)
    return dict(n1=n1, n2=n2, f1_full=c(f1, BF16), f1_half=c(f1[:, :n1 // 2], BF16), f3=c(f3, BF16),
                g_fwd=c(g_fwd, BF16), g_inv=c(g_inv, BF16), tw_r=c(tw_r, F32), tw_i=c(tw_i, F32))


def _dft_outer_fwd_kernel(f_ref, x_ref, o_ref):
    f = f_ref[...]
    for s in range(x_ref.shape[1]):
        o_ref[:, s, :] = jnp.dot(f, x_ref[:, s, :].astype(BF16), preferred_element_type=F32)


def _dft_outer_forward(fmat, x, seq0, nseq, *, ct=1024):
    _, k, n2, c = x.shape
    mo = fmat.shape[0]
    ct = min(ct, c)
    return pl.pallas_call(
        _dft_outer_fwd_kernel,
        grid=(nseq, n2 // SUBLANES, c // ct),
        in_specs=[
            pl.BlockSpec((mo, k), lambda b, s, j: (0, 0)),
            pl.BlockSpec((None, k, SUBLANES, ct), lambda b, s, j: (seq0 + b, 0, s, j)),
        ],
        out_specs=pl.BlockSpec((None, mo, SUBLANES, ct), lambda b, s, j: (b, 0, s, j)),
        out_shape=jax.ShapeDtypeStruct((nseq, mo, n2, c), F32),
        compiler_params=_params("parallel", "parallel", "parallel"),
        name="dft_outer_forward",
    )(fmat, x)


def _twiddle_dft(ar_ref, ai_ref, twr_ref, twi_ref, g_ref):
    n2 = ar_ref.shape[0]
    ar, ai = ar_ref[...], ai_ref[...]
    tr, ti = twr_ref[...], twi_ref[...]
    z = jnp.concatenate([ar * tr - ai * ti, ar * ti + ai * tr], axis=0).astype(BF16)
    x = jnp.dot(g_ref[...], z, preferred_element_type=F32)
    return x[:n2], x[n2:]


def _spectrum_kernel(ar_ref, ai_ref, twr_ref, twi_ref, g_ref, kr_ref, ki_ref, *, scale):
    xr, xi = _twiddle_dft(ar_ref, ai_ref, twr_ref, twi_ref, g_ref)
    kr_ref[...] = xr * scale
    ki_ref[...] = xi * scale


def _fft_mid_kernel(ar_ref, ai_ref, twr_ref, twi_ref, g_ref, gi_ref, kr_ref, ki_ref, br_ref, bi_ref):
    n2 = ar_ref.shape[0]
    xr, xi = _twiddle_dft(ar_ref, ai_ref, twr_ref, twi_ref, g_ref)
    kr, ki = kr_ref[...], ki_ref[...]
    y = jnp.concatenate([xr * kr - xi * ki, xr * ki + xi * kr], axis=0).astype(BF16)
    b = jnp.dot(gi_ref[...], y, preferred_element_type=F32)
    br, bi = b[:n2], b[n2:]
    tr, ti = twr_ref[...], twi_ref[...]
    br_ref[...] = br * tr + bi * ti
    bi_ref[...] = bi * tr - br * ti


def _fft_mid(a, consts, kf=None, *, ct=1024):
    nb, _, n2, c = a.shape
    n1 = consts["n1"]
    ct = min(ct, c)
    grid = (nb, n1, c // ct)
    blk = lambda im: pl.BlockSpec((None, None, n2, ct), lambda b, k, j: (b, im * n1 + k, 0, j))
    tw = pl.BlockSpec((None, n2, 1), lambda b, k, j: (k, 0, 0))
    gs = pl.BlockSpec((2 * n2, 2 * n2), lambda b, k, j: (0, 0))
    half_spec = pl.BlockSpec((None, None, n2, ct), lambda b, k, j: (b, k, 0, j))
    half_shape = jax.ShapeDtypeStruct((nb, n1, n2, c), F32)
    if kf is None:
        return pl.pallas_call(
            functools.partial(_spectrum_kernel, scale=1.0 / (n1 * n2)),
            grid=grid,
            in_specs=[blk(0), blk(1), tw, tw, gs],
            out_specs=[half_spec, half_spec],
            out_shape=[half_shape, half_shape],
            compiler_params=_params("parallel", "parallel", "parallel"),
            name="dft_filter_spectrum",
        )(a, a, consts["tw_r"], consts["tw_i"], consts["g_fwd"])
    kblk = lambda: pl.BlockSpec((None, None, n2, ct), lambda b, k, j: (0, k, 0, j))
    return pl.pallas_call(
        _fft_mid_kernel,
        grid=grid,
        in_specs=[blk(0), blk(1), tw, tw, gs, gs, kblk(), kblk()],
        out_specs=[half_spec, half_spec],
        out_shape=[half_shape, half_shape],
        compiler_params=_params("parallel", "parallel", "parallel"),
        name="dft_inner_filter",
    )(a, a, consts["tw_r"], consts["tw_i"], consts["g_fwd"], consts["g_inv"], kf[0], kf[1])


def _dft_outer_inv_kernel(f_ref, br_ref, bi_ref, u_ref, x0_ref, bias_ref, *rest):
    o_ref = rest[-1]
    f = f_ref[...]
    bias = bias_ref[...]
    for s in range(u_ref.shape[1]):
        b = jnp.concatenate([br_ref[:, s, :], bi_ref[:, s, :]], axis=0).astype(BF16)
        y = jnp.dot(f, b, preferred_element_type=F32)
        o_ref[:, s, :] = x0_ref[:, s, :] * (y + u_ref[:, s, :] * bias)


def _dft_outer_inverse(f3, br, bi, u, x0, bias, seq0, prev, *, ct=1024):
    nseq, n1, n2, c = br.shape
    mo = f3.shape[0]
    ct = min(ct, c)
    spec_in = pl.BlockSpec((None, n1, SUBLANES, ct), lambda b, s, j: (b, 0, s, j))
    spec_tok = pl.BlockSpec((None, mo, SUBLANES, ct), lambda b, s, j: (seq0 + b, 0, s, j))
    in_specs = [pl.BlockSpec(f3.shape, lambda b, s, j: (0, 0)), spec_in, spec_in, spec_tok, spec_tok,
                pl.BlockSpec((1, ct), lambda b, s, j: (0, j))]
    args = [f3, br, bi, u, x0, bias.reshape(1, c)]
    aliases = _alias_prev(in_specs, args, None if prev is None else prev.reshape(u.shape))
    return pl.pallas_call(
        _dft_outer_inv_kernel,
        grid=(nseq, n2 // SUBLANES, c // ct),
        in_specs=in_specs,
        out_specs=spec_tok,
        out_shape=jax.ShapeDtypeStruct(u.shape, F32),
        input_output_aliases=aliases,
        compiler_params=_params("parallel", "parallel", "parallel"),
        name="dft_outer_inverse",
    )(*args)


def _hyena_long_conv_group(u, x0, k_two, bias, row0, nseq, seq_len, prev):
    m, c = u.shape
    consts = _dft_constants(seq_len)
    n1, n2 = consts["n1"], consts["n2"]
    ak = _dft_outer_forward(consts["f1_full"], k_two.reshape(1, n1, n2, c), 0, 1)
    kfr, kfi = _fft_mid(ak, consts)
    tok_view = (m // seq_len, n1 // 2, n2, c)
    seq0 = row0 // seq_len
    a = _dft_outer_forward(consts["f1_half"], u.reshape(tok_view), seq0, nseq)
    br, bi = _fft_mid(a, consts, kf=(kfr, kfi))
    y = _dft_outer_inverse(consts["f3"], br, bi, u.reshape(tok_view), x0.reshape(tok_view), bias, seq0, prev)
    return y.reshape(m, c)


def _softplus(z):
    return jnp.maximum(z, 0.0) + jnp.log(1.0 + jnp.exp(-jnp.abs(z)))


def _lru_kernel(pl_ref, pg_ref, cw_ref, cb_ref, wa_ref, ba_ref, wx_ref, bx_ref, lam_ref, *rest, tchunk):
    o_ref, xp_ref, hs_ref = rest[-3:]
    seq_len, cw = pl_ref.shape
    nchunk = seq_len // tchunk
    halo = SUBLANES
    xp_ref[0:halo, :] = jnp.zeros((halo, cw), F32)
    xp_ref[halo + seq_len:2 * halo + seq_len, :] = jnp.zeros((halo, cw), F32)
    xp_ref[halo:halo + seq_len, :] = pl_ref[...]

    row = lax.broadcasted_iota(jnp.int32, (tchunk, cw), 0)
    nsteps = int(math.log2(tchunk))
    neg_c_softplus = -LRU_C * _softplus(-lam_ref[...])

    def gates(c, direction):
        base = pl.multiple_of(c * tchunk, tchunk)
        xw = xp_ref[pl.ds(base, tchunk + 2 * halo), :]
        ext = tchunk + 2 * halo
        xb = cb_ref[...]
        for j in range(LRU_CONV):
            shift = (LRU_CONV // 2 - j) % ext
            tap = xw if shift == 0 else pltpu.roll(xw, shift, 0)
            xb = xb + tap[halo:halo + tchunk, :] * cw_ref[j:j + 1, :]
        xb16 = xb.astype(BF16)
        r = jax.nn.sigmoid(jnp.dot(xb16, wa_ref[direction], preferred_element_type=F32)
                           + ba_ref[direction:direction + 1, :])
        i = jax.nn.sigmoid(jnp.dot(xb16, wx_ref[direction], preferred_element_type=F32)
                           + bx_ref[direction:direction + 1, :])
        log_a = r * neg_c_softplus[direction:direction + 1, :]
        a = jnp.exp(log_a)
        gain = jnp.sqrt(-jnp.tanh(log_a) * (a * a + 1.0))
        return a, gain * (i * xb)

    def scan(a, b, reverse):
        for s in range(nsteps):
            d = 1 << s
            if reverse:
                keep = row < tchunk - d
                a_sh = pltpu.roll(a, tchunk - d, 0)
                b_sh = pltpu.roll(b, tchunk - d, 0)
            else:
                keep = row >= d
                a_sh = pltpu.roll(a, d, 0)
                b_sh = pltpu.roll(b, d, 0)
            b = jnp.where(keep, a * b_sh + b, b)
            a = jnp.where(keep, a * a_sh, a)
        return a, b

    def fwd_body(c, h):
        a, b = gates(c, 0)
        a, b = scan(a, b, False)
        hc = a * h + b
        hs_ref[pl.ds(pl.multiple_of(c * tchunk, tchunk), tchunk), :] = hc
        return hc[tchunk - 1:tchunk, :]

    lax.fori_loop(0, nchunk, fwd_body, jnp.zeros((1, cw), F32), unroll=min(4, nchunk))

    def rev_body(cc, h):
        c = nchunk - 1 - cc
        a, b = gates(c, 1)
        a, b = scan(a, b, True)
        hc = a * h + b
        sl = pl.ds(pl.multiple_of(c * tchunk, tchunk), tchunk)
        o_ref[sl, :] = (hs_ref[sl, :] + hc) * jax.nn.gelu(pg_ref[sl, :])
        return hc[0:1, :]

    lax.fori_loop(0, nchunk, rev_body, jnp.zeros((1, cw), F32), unroll=min(4, nchunk))


def _rg_lru_group(p, col_gate, col_x, conv_w, conv_b, wa, ba, wx, bx, lam, row0, nseq, seq_len, prev, *,
                  lead=(), tchunk=128):
    m = p.shape[0]
    nblk = wa.shape[-3]
    cw = wa.shape[-1]
    tchunk = min(tchunk, seq_len)
    sb0 = row0 // seq_len
    gb0 = col_gate // cw
    xb0 = col_x // cw
    vec = lambda rows: pl.BlockSpec((rows, cw), lambda b, n: (0, n))
    mat = _stacked_spec(lead, (2, None, cw, cw), lambda b, n: (0, n, 0, 0))
    in_specs = [
        pl.BlockSpec((seq_len, cw), lambda b, n: (sb0 + b, xb0 + n)),
        pl.BlockSpec((seq_len, cw), lambda b, n: (sb0 + b, gb0 + n)),
        vec(LRU_CONV), vec(1), mat, vec(2), mat, vec(2), vec(2),
    ]
    args = [p, p, conv_w, conv_b.reshape(1, -1), wa, ba, wx, bx, lam]
    aliases = _alias_prev(in_specs, args, prev)
    return pl.pallas_call(
        functools.partial(_lru_kernel, tchunk=tchunk),
        grid=(nseq, nblk),
        in_specs=in_specs,
        out_specs=pl.BlockSpec((seq_len, cw), lambda b, n: (sb0 + b, n)),
        out_shape=jax.ShapeDtypeStruct((m, nblk * cw), F32),
        input_output_aliases=aliases,
        scratch_shapes=[pltpu.VMEM((seq_len + 2 * SUBLANES, cw), F32), pltpu.VMEM((seq_len, cw), F32)],
        compiler_params=_params("parallel", "parallel"),
        name="rg_lru",
    )(*args)


def kernel(x_prompt, x_sample, ffn_norm, ffn_w1, ffn_w3, ffn_w2, mix_norm, final_norm, even_w_in, hyena_conv_w, hyena_conv_b, hyena_filt_w_in, hyena_filt_b, hyena_filt_w_hid, hyena_filt_freq, hyena_filt_w_out, hyena_bias, lru_conv_w, lru_conv_b, lru_wa, lru_ba, lru_wx, lru_bx, lru_lambda, even_w_out, attn_w_qkv, attn_lambda, attn_subln, attn_w_o):
    d = x_prompt.shape[-1]
    depth = ffn_norm.shape[0]
    groups = []
    row0 = 0
    for xg in (x_prompt, x_sample):
        groups.append((row0, xg.shape[0], xg.shape[1]))
        row0 += xg.shape[0] * xg.shape[1]
    groups = tuple(groups)
    m = row0
    inputs = (x_prompt, x_sample)

    hw = hyena_bias.shape[1]
    head_dim = d // (2 * ATT_HEADS)
    rope = _rope_tables(max(l for _, _, l in groups), head_dim)
    w1, w3, w2 = ffn_w1.astype(BF16), ffn_w3.astype(BF16), ffn_w2.astype(BF16)
    w_in, w_out = even_w_in.astype(BF16), even_w_out.astype(BF16)
    w_qkv, w_o = _permute_qk_columns(attn_w_qkv.astype(BF16), d, head_dim), attn_w_o.astype(BF16)
    wa, wx = lru_wa.astype(BF16), lru_wx.astype(BF16)
    lru_cols = (3 * hw, 3 * hw + lru_wa.shape[-3] * lru_wa.shape[-1])

    x = jnp.zeros((m, d), F32)
    for i in range(depth):
        j = i // 2
        if i == 0:
            for (row0, _, _), xg in zip(groups, inputs):
                x = _ffn(xg.reshape(-1, d), ffn_norm[i, 0], w1, w3, w2, lead=(i, 0), out_rows=m, out_row0=row0,
                         prev=x)
        else:
            x = _ffn(x, ffn_norm[i, 0], w1, w3, w2, lead=(i, 0))
        if i % 2 == 0:
            p = _norm_matmul(x, mix_norm[i], w_in, lead=(j,))
            x0, u = _hyena_pre(p, hyena_conv_w[j], hyena_conv_b[j], groups, hw)
            yh = jnp.zeros((m, hw), F32)
            yl = jnp.zeros((m, lru_cols[1] - lru_cols[0]), F32)
            for row0, nseq, seq_len in groups:
                k_two = _hyena_filter(seq_len, hyena_filt_w_in[j], hyena_filt_b[j], hyena_filt_w_hid[j],
                                      hyena_filt_freq[j], hyena_filt_w_out[j])
                yh = _hyena_long_conv_group(u, x0, k_two, hyena_bias[j], row0, nseq, seq_len, yh)
                yl = _rg_lru_group(p, *lru_cols, lru_conv_w[j], lru_conv_b[j], wa, lru_ba[j], wx, lru_bx[j],
                                   lru_lambda[j], row0, nseq, seq_len, yl, lead=(j,))
            x = _proj_residual(x, [yh, yl], w_out, lead=(j,))
        else:
            lambda_init = 0.8 - 0.6 * math.exp(-0.3 * i)
            qkv = _qkv(x, mix_norm[i], w_qkv, rope, groups, lead=(j,))
            o = jnp.zeros((m, d), BF16)
            for row0, nseq, seq_len in groups:
                o = _diff_attention_group(qkv, attn_lambda[j], attn_subln[j], row0, nseq, seq_len, lambda_init, o)
            x = _proj_residual(x, [o], w_o, lead=(j,))
        if i + 1 < depth:
            x = _ffn(x, ffn_norm[i, 1], w1, w3, w2, lead=(i, 1))

    outs = []
    for (row0, nseq, seq_len), xg in zip(groups, inputs):
        y = _ffn(x, ffn_norm[depth - 1, 1], w1, w3, w2, lead=(depth - 1, 1), in_row0=row0, nrows=nseq * seq_len,
                 final_g=final_norm)
        outs.append(y.reshape(xg.shape))
    return tuple(outs)
```
